```python
import math
import jax, jax.numpy as jnp
from jax import lax
import numpy as np

D_MODEL = 4096
BATCH = 2
SEQ = 8192
DEPTH = 4

CHUNK = 64
N_MEM = 256

A_HEADS = 8
A_DK = 128
A_DV = 128
A_WIDTH = A_HEADS * A_DV
B_HEADS = 12
B_DK = 128
B_DV = 128
B_WIDTH = B_HEADS * B_DV
B_CONV = 4
B_QKV = 2 * B_HEADS * B_DK + B_WIDTH
C_HEADS = 6
C_DK = 128
C_DV = 256
C_WIDTH = C_HEADS * C_DV
C_RANK = 16
C_TAU = 16.0
MIX_WIDTH = A_WIDTH + B_WIDTH + C_WIDTH

X_HEADS = 4
X_HEAD_DIM = 128
X_WIDTH = X_HEADS * X_HEAD_DIM

N_EXPERTS = 32
TOP_K = 4
EXPERT_FF = 256
SWIGLU_LIMIT = 7.0
SWIGLU_ALPHA = 1.702

DEEPNORM_ALPHA = (2 * DEPTH) ** 0.25
DEEPNORM_BETA = (8 * DEPTH) ** -0.25
LN_EPS = 1e-5
RMS_EPS = 1e-6

IN_SPLITS = (
    A_HEADS * A_DK,
    A_HEADS * A_DK,
    A_WIDTH,
    A_WIDTH,
    B_QKV,
    B_WIDTH,
    B_HEADS,
    B_HEADS,
    C_HEADS * C_DK,
    C_HEADS * C_DK,
    C_WIDTH,
    C_WIDTH,
    C_RANK,
)
IN_COLS = sum(IN_SPLITS)
IN_OFFSETS = tuple(int(v) for v in np.cumsum(IN_SPLITS)[:-1])

kernel_name = "hybrid_hgrn2_gdn_gla_moe_encoder"


def _layer_norm(x, g, b):
    xf = x.astype(jnp.float32)
    mu = jnp.mean(xf, axis=-1, keepdims=True)
    var = jnp.mean(jnp.square(xf - mu), axis=-1, keepdims=True)
    y = (xf - mu) * lax.rsqrt(var + LN_EPS)
    return (y * g + b).astype(x.dtype)


def _rms_norm(x, g):
    xf = x.astype(jnp.float32)
    y = xf * lax.rsqrt(jnp.mean(jnp.square(xf), axis=-1, keepdims=True) + RMS_EPS)
    return (y * g).astype(x.dtype)


def _l2_normalize(x):
    xf = x.astype(jnp.float32)
    return (xf * lax.rsqrt(jnp.sum(jnp.square(xf), axis=-1, keepdims=True) + RMS_EPS)).astype(x.dtype)


def _heads(t, n_heads):
    bsz, seq, _ = t.shape
    return t.reshape(bsz, seq, n_heads, -1).transpose(0, 2, 1, 3)


def _merge_heads(t):
    bsz, n_heads, seq, d = t.shape
    return t.transpose(0, 2, 1, 3).reshape(bsz, seq, n_heads * d)


def _causal_depthwise_conv(x, w):
    k = w.shape[0]
    return lax.conv_general_dilated(
        x, w[:, None, :], window_strides=(1,), padding=[(k - 1, 0)],
        dimension_numbers=("NWC", "WIO", "NWC"), feature_group_count=x.shape[-1])


def _diag_gated_chunked(q, k, v, log_a):
    out_dtype = v.dtype
    bsz, n_heads, seq, dk = q.shape
    dv = v.shape[-1]
    nc = seq // CHUNK

    def chunks(t):
        t = t.astype(jnp.float32)
        return t.reshape(bsz, n_heads, nc, CHUNK, t.shape[-1]).transpose(2, 0, 1, 3, 4)

    qc, kc, vc = chunks(q), chunks(k), chunks(v)
    bc = jnp.cumsum(chunks(log_a), axis=3)
    causal = jnp.tril(jnp.ones((CHUNK, CHUNK), dtype=bool))[:, :, None]

    def step(state, xs):
        qi, ki, vi, bi = xs
        rel = jnp.exp(jnp.where(causal, bi[:, :, :, None, :] - bi[:, :, None, :, :], -jnp.inf))
        scores = jnp.sum(qi[:, :, :, None, :] * ki[:, :, None, :, :] * rel, axis=-1)
        o = scores @ vi + (qi * jnp.exp(bi)) @ state
        last = bi[:, :, -1:, :]
        state = jnp.exp(last).swapaxes(-1, -2) * state + (ki * jnp.exp(last - bi)).swapaxes(-1, -2) @ vi
        return state, o

    s0 = jnp.zeros((bsz, n_heads, dk, dv), jnp.float32)
    _, o = lax.scan(step, s0, (qc, kc, vc, bc))
    return o.transpose(1, 2, 0, 3, 4).reshape(bsz, n_heads, seq, dv).astype(out_dtype)


def _gated_delta_chunked(q, k, v, beta, g):
    out_dtype = v.dtype
    bsz, n_heads, seq, dk = q.shape
    dv = v.shape[-1]
    nc = seq // CHUNK
    qc = q.astype(jnp.float32).reshape(bsz, n_heads, nc, CHUNK, dk)
    kc = k.astype(jnp.float32).reshape(bsz, n_heads, nc, CHUNK, dk)
    vc = v.astype(jnp.float32).reshape(bsz, n_heads, nc, CHUNK, dv)
    bc = beta.astype(jnp.float32).reshape(bsz, n_heads, nc, CHUNK)
    gc = jnp.cumsum(g.astype(jnp.float32).reshape(bsz, n_heads, nc, CHUNK), axis=-1)

    incl = jnp.tril(jnp.ones((CHUNK, CHUNK), dtype=bool))
    strict = jnp.tril(jnp.ones((CHUNK, CHUNK), dtype=bool), -1)
    gamma = jnp.exp(jnp.where(incl, gc[..., :, None] - gc[..., None, :], -jnp.inf))
    kk = jnp.einsum("bhntd,bhnsd->bhnts", kc, kc)
    m = jnp.where(strict, bc[..., :, None] * kk * gamma, 0.0)
    eye = jnp.eye(CHUNK, dtype=jnp.float32)
    rhs = jnp.concatenate([vc * bc[..., None], kc * (bc * jnp.exp(gc))[..., None]], axis=-1)
    sol = lax.linalg.triangular_solve(eye + m, rhs, left_side=True, lower=True, unit_diagonal=True)
    u, w = sol[..., :dv], sol[..., dv:]
    attn = jnp.einsum("bhntd,bhnsd->bhnts", qc, kc) * gamma
    q_dec = qc * jnp.exp(gc)[..., None]
    k_dec = kc * jnp.exp(gc[..., -1:] - gc)[..., None]
    g_last = jnp.exp(gc[..., -1])

    def step(state, xs):
        q_i, k_i, u_i, w_i, a_i, gl = xs
        v_new = u_i - w_i @ state
        o = q_i @ state + a_i @ v_new
        state = gl[..., None, None] * state + k_i.swapaxes(-1, -2) @ v_new
        return state, o

    s0 = jnp.zeros((bsz, n_heads, dk, dv), jnp.float32)
    xs = (jnp.moveaxis(q_dec, 2, 0), jnp.moveaxis(k_dec, 2, 0), jnp.moveaxis(u, 2, 0),
          jnp.moveaxis(w, 2, 0), jnp.moveaxis(attn, 2, 0), jnp.moveaxis(g_last, 2, 0))
    _, o = lax.scan(step, s0, xs)
    return jnp.moveaxis(o, 0, 2).reshape(bsz, n_heads, seq, dv).astype(out_dtype)


def _hybrid_mixer(x, w_in, lb, hgrn_norm_g, conv_w, a_log, dt_bias, gdn_norm_g,
                  gla_w_up, gla_b_up, gla_norm_g, w_out):
    proj = x @ w_in
    (aq, af, ai, ag, bqkv, bz, bb, ba, cq, ck, cv, cr, clr) = jnp.split(proj, IN_OFFSETS, axis=-1)

    f = lb + (1.0 - lb) * jax.nn.sigmoid(af.astype(jnp.float32))
    oa = _diag_gated_chunked(_heads(jax.nn.silu(aq), A_HEADS), _heads(1.0 - f, A_HEADS),
                             _heads(ai, A_HEADS), _heads(jnp.log(f), A_HEADS))
    oa = _rms_norm(oa, hgrn_norm_g) * jax.nn.silu(_heads(ag, A_HEADS))

    qkv = jax.nn.silu(_causal_depthwise_conv(bqkv, conv_w))
    bq, bk, bv = jnp.split(qkv, [B_HEADS * B_DK, 2 * B_HEADS * B_DK], axis=-1)
    qb = _l2_normalize(_heads(bq, B_HEADS)) * (B_DK ** -0.5)
    kb = _l2_normalize(_heads(bk, B_HEADS))
    beta = jax.nn.sigmoid(bb.astype(jnp.float32)).transpose(0, 2, 1)
    gdec = (-jnp.exp(a_log.astype(jnp.float32))
            * jax.nn.softplus(ba.astype(jnp.float32) + dt_bias.astype(jnp.float32))).transpose(0, 2, 1)
    ob = _gated_delta_chunked(qb, kb, _heads(bv, B_HEADS), beta, gdec)
    ob = _rms_norm(ob, gdn_norm_g) * jax.nn.silu(_heads(bz, B_HEADS))

    log_alpha = jax.nn.log_sigmoid((clr @ gla_w_up + gla_b_up).astype(jnp.float32)) / C_TAU
    oc = _diag_gated_chunked(_heads(cq, C_HEADS) * (C_DK ** -0.5), _heads(ck, C_HEADS),
                             _heads(cv, C_HEADS), _heads(log_alpha, C_HEADS))
    oc = _rms_norm(oc, gla_norm_g) * jax.nn.silu(_heads(cr, C_HEADS))

    o = jnp.concatenate([_merge_heads(oa), _merge_heads(ob), _merge_heads(oc)], axis=-1)
    return o @ w_out


def _memory_cross_attention(x, mem_n, wq, wkv, wo):
    bsz, seq, _ = x.shape
    n_mem = mem_n.shape[1]
    q = (x @ wq).reshape(bsz, seq, X_HEADS, X_HEAD_DIM)
    kv = (mem_n @ wkv).reshape(bsz, n_mem, 2, X_HEADS, X_HEAD_DIM)
    k, v = kv[:, :, 0], kv[:, :, 1]
    s = jnp.einsum("bthd,bmhd->bhtm", q, k).astype(jnp.float32) * (X_HEAD_DIM ** -0.5)
    p = jax.nn.softmax(s, axis=-1).astype(x.dtype)
    o = jnp.einsum("bhtm,bmhd->bthd", p, v).reshape(bsz, seq, X_WIDTH)
    return o @ wo


def _moe(x, w_router, b_router, w_gate, b_gate, w_up, b_up, w_down, b_down):
    bsz, seq, d = x.shape
    xt = x.reshape(-1, d)
    logits = (xt @ w_router + b_router).astype(jnp.float32)
    top_logit, top_idx = lax.top_k(logits, TOP_K)
    top_w = jax.nn.softmax(top_logit, axis=-1)
    comb = jnp.einsum("nk,nke->ne", top_w,
                      jax.nn.one_hot(top_idx, N_EXPERTS, dtype=jnp.float32)).astype(x.dtype)
    gate = jnp.minimum(jnp.einsum("nd,edf->nef", xt, w_gate) + b_gate, SWIGLU_LIMIT)
    lin = jnp.clip(jnp.einsum("nd,edf->nef", xt, w_up) + b_up, -SWIGLU_LIMIT, SWIGLU_LIMIT)
    h = (lin + 1.0) * gate * jax.nn.sigmoid(SWIGLU_ALPHA * gate)
    y = jnp.einsum("nef,efd->nd", h * comb[..., None], w_down) + comb @ b_down
    return y.reshape(bsz, seq, d)


def setup_inputs(seed: int = 0) -> dict:
    key = jax.random.key(seed)
    ks = iter(jax.random.split(key, 48))
    L, D, E, F = DEPTH, D_MODEL, N_EXPERTS, EXPERT_FF

    def nrm(shape, scale):
        return jax.random.normal(next(ks), shape, jnp.float32) * scale

    def gain(shape):
        return 1.0 + nrm(shape, 0.02)

    x = nrm((BATCH, SEQ, D), 1.0)
    mem = nrm((BATCH, N_MEM, D), 1.0)
    w_in = nrm((L, D, IN_COLS), D ** -0.5)
    hgrn_lb_raw = nrm((L, A_HEADS * A_DK), 0.5)
    hgrn_norm_g = gain((L, A_DV))
    gdn_conv_w = nrm((L, B_CONV, B_QKV), B_CONV ** -0.5)
    gdn_a_log = jnp.log(jax.random.uniform(next(ks), (L, B_HEADS), jnp.float32, 1.0, 16.0))
    dt = jnp.exp(jax.random.uniform(next(ks), (L, B_HEADS), jnp.float32, math.log(1e-3), math.log(1e-1)))
    gdn_dt_bias = dt + jnp.log(-jnp.expm1(-dt))
    gdn_norm_g = gain((L, B_DV))
    gla_w_up = nrm((L, C_RANK, C_HEADS * C_DK), C_RANK ** -0.5)
    gla_b_up = nrm((L, C_HEADS * C_DK), 0.1)
    gla_norm_g = gain((L, C_DV))
    w_out = nrm((L, MIX_WIDTH, D), MIX_WIDTH ** -0.5 * DEEPNORM_BETA)
    ln_mix_g = gain((L, D))
    ln_mix_b = nrm((L, D), 0.02)
    mem_ln_g = gain((D,))
    mem_ln_b = nrm((D,), 0.02)
    xattn_wq = nrm((L, D, X_WIDTH), D ** -0.5)
    xattn_wkv = nrm((L, D, 2 * X_WIDTH), D ** -0.5)
    xattn_wo = nrm((L, X_WIDTH, D), X_WIDTH ** -0.5 * DEEPNORM_BETA)
    ln_xattn_g = gain((L, D))
    ln_xattn_b = nrm((L, D), 0.02)
    w_router = nrm((L, D, E), D ** -0.5)
    b_router = nrm((L, E), 0.01)
    w_gate = nrm((L, E, D, F), D ** -0.5)
    b_gate = nrm((L, E, F), 0.02)
    w_up = nrm((L, E, D, F), D ** -0.5)
    b_up = nrm((L, E, F), 0.02)
    w_down = nrm((L, E, F, D), F ** -0.5 * DEEPNORM_BETA)
    b_down = nrm((L, E, D), 0.02)
    ln_ffn_g = gain((L, D))
    ln_ffn_b = nrm((L, D), 0.02)
    return {
        "x": x, "mem": mem, "w_in": w_in, "hgrn_lb_raw": hgrn_lb_raw, "hgrn_norm_g": hgrn_norm_g,
        "gdn_conv_w": gdn_conv_w, "gdn_a_log": gdn_a_log, "gdn_dt_bias": gdn_dt_bias,
        "gdn_norm_g": gdn_norm_g, "gla_w_up": gla_w_up, "gla_b_up": gla_b_up, "gla_norm_g": gla_norm_g,
        "w_out": w_out, "ln_mix_g": ln_mix_g, "ln_mix_b": ln_mix_b, "mem_ln_g": mem_ln_g,
        "mem_ln_b": mem_ln_b, "xattn_wq": xattn_wq, "xattn_wkv": xattn_wkv, "xattn_wo": xattn_wo,
        "ln_xattn_g": ln_xattn_g, "ln_xattn_b": ln_xattn_b, "w_router": w_router, "b_router": b_router,
        "w_gate": w_gate, "b_gate": b_gate, "w_up": w_up, "b_up": b_up, "w_down": w_down,
        "b_down": b_down, "ln_ffn_g": ln_ffn_g, "ln_ffn_b": ln_ffn_b,
    }


def reference(x, mem, w_in, hgrn_lb_raw, hgrn_norm_g, gdn_conv_w, gdn_a_log, gdn_dt_bias,
              gdn_norm_g, gla_w_up, gla_b_up, gla_norm_g, w_out, ln_mix_g, ln_mix_b, mem_ln_g,
              mem_ln_b, xattn_wq, xattn_wkv, xattn_wo, ln_xattn_g, ln_xattn_b, w_router, b_router,
              w_gate, b_gate, w_up, b_up, w_down, b_down, ln_ffn_g, ln_ffn_b):
    lb_all = jnp.cumsum(jax.nn.softmax(hgrn_lb_raw.astype(jnp.float32), axis=0), axis=0)
    mem_n = _layer_norm(mem, mem_ln_g, mem_ln_b)
    for l in range(DEPTH):
        h = _hybrid_mixer(x, w_in[l], lb_all[l] - lb_all[0], hgrn_norm_g[l], gdn_conv_w[l],
                          gdn_a_log[l], gdn_dt_bias[l], gdn_norm_g[l], gla_w_up[l], gla_b_up[l],
                          gla_norm_g[l], w_out[l])
        x = _layer_norm(DEEPNORM_ALPHA * x + h, ln_mix_g[l], ln_mix_b[l])
        h = _memory_cross_attention(x, mem_n, xattn_wq[l], xattn_wkv[l], xattn_wo[l])
        x = _layer_norm(DEEPNORM_ALPHA * x + h, ln_xattn_g[l], ln_xattn_b[l])
        h = _moe(x, w_router[l], b_router[l], w_gate[l], b_gate[l], w_up[l], b_up[l],
                 w_down[l], b_down[l])
        x = _layer_norm(DEEPNORM_ALPHA * x + h, ln_ffn_g[l], ln_ffn_b[l])
    return x
```

```python
import functools

import jax
import jax.numpy as jnp
from jax import lax
from jax.experimental import pallas as pl
from jax.experimental.pallas import tpu as pltpu

F32 = jnp.float32
BF16 = jnp.bfloat16

LANES = 128
CHUNK = 64
VMEM_LIMIT = 56 * 1024 * 1024

A_HEADS, B_HEADS, C_HEADS = 8, 12, 6
HEAD_DK = 128
A_DV, B_DV, C_DV = 128, 128, 256
B_CONV = 4
C_RANK = 16
C_TAU = 16.0
X_HEADS, X_HEAD_DIM = 4, 128
N_EXPERTS, TOP_K, EXPERT_FF = 32, 4, 256
SWIGLU_LIMIT, SWIGLU_ALPHA = 7.0, 1.702
LN_EPS, RMS_EPS = 1e-5, 1e-6

COL_AQ, COL_AF, COL_AI, COL_AG = 0, 8, 16, 24
COL_BQ, COL_BK, COL_BV, COL_BZ = 32, 44, 56, 68
COL_CQ, COL_CK, COL_CV, COL_CR = 80, 86, 92, 104
COL_MISC = 116
MISC_BETA, MISC_DECAY, MISC_CODE = 0, 12, 24
IN_COLS_PAD = 120 * LANES


def _cparams(sem):
    return pltpu.CompilerParams(dimension_semantics=sem, vmem_limit_bytes=VMEM_LIMIT)


def _dot(a, b):
    return jnp.dot(a, b, preferred_element_type=F32)


def _dot_nt(a, b):
    return lax.dot_general(a, b, (((1,), (1,)), ((), ())), preferred_element_type=F32)


def _dot_tn(a, b):
    return lax.dot_general(a, b, (((0,), (0,)), ((), ())), preferred_element_type=F32)


def _split(x):
    hi = x.astype(BF16)
    lo = (x - hi.astype(F32)).astype(BF16)
    return hi, lo


def _dot_exact_lhs(sel, x):
    hi, lo = _split(x)
    return _dot(sel, hi) + _dot(sel, lo)


def _sigmoid(x):
    return 1.0 / (1.0 + jnp.exp(-x))


def _silu(x):
    return x * _sigmoid(x)


def _iota(shape, dim):
    return lax.broadcasted_iota(jnp.int32, shape, dim)


def _tri_incl():
    return (_iota((CHUNK, CHUNK), 0) >= _iota((CHUNK, CHUNK), 1))


def _mm_kernel(a_ref, w_ref, o_ref):
    o_ref[...] = _dot(a_ref[...], w_ref[...]).astype(o_ref.dtype)


def _matmul(a, w, out_dtype, tm, tn):
    m, k = a.shape
    n = w.shape[1]
    return pl.pallas_call(
        _mm_kernel,
        grid=(n // tn, m // tm),
        in_specs=[pl.BlockSpec((tm, k), lambda j, i: (i, 0)),
                  pl.BlockSpec((k, tn), lambda j, i: (0, j))],
        out_specs=pl.BlockSpec((tm, tn), lambda j, i: (i, j)),
        out_shape=jax.ShapeDtypeStruct((m, n), out_dtype),
        compiler_params=_cparams(("arbitrary", "arbitrary")),
        name="matmul",
    )(a, w)


def _ln_rows(y, g, b):
    mu = jnp.mean(y, axis=-1, keepdims=True)
    d = y - mu
    var = jnp.mean(d * d, axis=-1, keepdims=True)
    return d * lax.rsqrt(var + LN_EPS) * g + b


def _ln_kernel(x_ref, g_ref, b_ref, o_ref):
    o_ref[...] = _ln_rows(x_ref[...], g_ref[...], b_ref[...]).astype(o_ref.dtype)


def _layer_norm(x, g, b, out_dtype, tm):
    m, d = x.shape
    return pl.pallas_call(
        _ln_kernel,
        grid=(m // tm,),
        in_specs=[pl.BlockSpec((tm, d), lambda i: (i, 0)),
                  pl.BlockSpec((1, d), lambda i: (0, 0)),
                  pl.BlockSpec((1, d), lambda i: (0, 0))],
        out_specs=pl.BlockSpec((tm, d), lambda i: (i, 0)),
        out_shape=jax.ShapeDtypeStruct((m, d), out_dtype),
        compiler_params=_cparams(("arbitrary",)),
        name="layer_norm",
    )(x, g.reshape(1, d), b.reshape(1, d))


def _proj_ln_kernel(a_ref, w_ref, x_ref, g_ref, b_ref, of_ref, ob_ref, *, alpha, nk):
    kk = pl.program_id(1)
    part = _dot(a_ref[...], w_ref[...])

    @pl.when(kk == 0)
    def _():
        of_ref[...] = part

    @pl.when(kk > 0)
    def _():
        of_ref[...] += part

    @pl.when(kk == nk - 1)
    def _():
        y = _ln_rows(alpha * x_ref[...] + of_ref[...], g_ref[...], b_ref[...])
        of_ref[...] = y
        ob_ref[...] = y.astype(BF16)


def _proj_ln(a, w, x, g, b, alpha, tm, tk):
    m, k = a.shape
    d = w.shape[1]
    nk = k // tk
    return pl.pallas_call(
        functools.partial(_proj_ln_kernel, alpha=alpha, nk=nk),
        grid=(m // tm, nk),
        in_specs=[pl.BlockSpec((tm, tk), lambda i, kk: (i, kk)),
                  pl.BlockSpec((tk, d), lambda i, kk: (kk, 0)),
                  pl.BlockSpec((tm, d), lambda i, kk: (i, 0)),
                  pl.BlockSpec((1, d), lambda i, kk: (0, 0)),
                  pl.BlockSpec((1, d), lambda i, kk: (0, 0))],
        out_specs=[pl.BlockSpec((tm, d), lambda i, kk: (i, 0)),
                   pl.BlockSpec((tm, d), lambda i, kk: (i, 0))],
        out_shape=[jax.ShapeDtypeStruct((m, d), F32), jax.ShapeDtypeStruct((m, d), BF16)],
        compiler_params=_cparams(("arbitrary", "arbitrary")),
        name="proj_ln",
    )(a, w, x, g.reshape(1, d), b.reshape(1, d))


_LEVELS = (32, 16, 8, 4, 2, 1)


def _level_selectors():
    r = _iota((CHUNK, CHUNK), 0)
    c = _iota((CHUNK, CHUNK), 1)
    sels = []
    for s in _LEVELS:
        boundary = (r & ~(2 * s - 1)) | (s - 1)
        sels.append(jnp.where(c == boundary, 1.0, 0.0))
    return jnp.concatenate(sels, axis=0).astype(BF16)


def _diag_gated_chunk(q, k, v, la, state_t):
    dk = q.shape[1]
    tri = jnp.where(_tri_incl(), 1.0, 0.0).astype(BF16)
    b = _dot_exact_lhs(tri, la)
    refs = _dot(_level_selectors(), b.astype(BF16))
    row = _iota((CHUNK, dk), 0)
    ri = _iota((CHUNK, CHUNK), 0)
    ci = _iota((CHUNK, CHUNK), 1)
    qb, kb, vb = q.astype(BF16), k.astype(BF16), v.astype(BF16)
    scores = jnp.where(ri == ci, _dot_nt(qb, kb), 0.0)
    for n, s in enumerate(_LEVELS):
        ref = refs[n * CHUNK:(n + 1) * CHUNK]
        later = (row & s) != 0
        qd = jnp.where(later, q * jnp.exp(jnp.where(later, b - ref, 0.0)), 0.0)
        kd = jnp.where(later, 0.0, k * jnp.exp(jnp.where(later, 0.0, ref - b)))
        sib = (ri >> (s.bit_length())) == (ci >> (s.bit_length()))
        scores = scores + jnp.where(sib, _dot_nt(qd.astype(BF16), kd.astype(BF16)), 0.0)
    b_last = b[CHUNK - 1:CHUNK, :]
    st_b = state_t.astype(BF16)
    o = _dot(scores.astype(BF16), vb) + _dot_nt((q * jnp.exp(b)).astype(BF16), st_b)
    k_dec = (k * jnp.exp(b_last - b)).astype(BF16)
    new_state_t = state_t * jnp.exp(b_last) + _dot_tn(vb, k_dec)
    return o, new_state_t


def _rms_gate(o, g, gate):
    y = o * lax.rsqrt(jnp.mean(o * o, axis=-1, keepdims=True) + RMS_EPS)
    return y * g * _silu(gate)


def _hgrn_kernel(q_ref, f_ref, i_ref, g_ref, lb_ref, ng_ref, o_ref, st_ref, *, nchunks):
    @pl.when(pl.program_id(2) == 0)
    def _():
        st_ref[...] = jnp.zeros_like(st_ref)

    lb = lb_ref[0]
    ng = ng_ref[...]

    def body(c, carry):
        sl = pl.ds(pl.multiple_of(c * CHUNK, CHUNK), CHUNK)
        f = lb + (1.0 - lb) * _sigmoid(f_ref[sl, :])
        o, st = _diag_gated_chunk(_silu(q_ref[sl, :]), 1.0 - f, i_ref[sl, :], jnp.log(f), st_ref[...])
        st_ref[...] = st
        o_ref[sl, :] = _rms_gate(o, ng, g_ref[sl, :]).astype(o_ref.dtype)
        return carry

    lax.fori_loop(0, nchunks, body, 0)


def _hgrn(proj, lb, norm_g, bsz, seq, tblk):
    nt = seq // tblk

    def col(base):
        return pl.BlockSpec((tblk, LANES), lambda b, h, t: (b * nt + t, base + h))

    return pl.pallas_call(
        functools.partial(_hgrn_kernel, nchunks=tblk // CHUNK),
        grid=(bsz, A_HEADS, nt),
        in_specs=[col(COL_AQ), col(COL_AF), col(COL_AI), col(COL_AG),
                  pl.BlockSpec((1, 1, LANES), lambda b, h, t: (h, 0, 0)),
                  pl.BlockSpec((1, A_DV), lambda b, h, t: (0, 0))],
        out_specs=pl.BlockSpec((tblk, A_DV), lambda b, h, t: (b * nt + t, h)),
        out_shape=jax.ShapeDtypeStruct((bsz * seq, A_HEADS * A_DV), BF16),
        scratch_shapes=[pltpu.VMEM((A_DV, HEAD_DK), F32)],
        compiler_params=_cparams(("arbitrary", "arbitrary", "arbitrary")),
        name="hgrn2",
    )(proj, proj, proj, proj, lb, norm_g.reshape(1, A_DV))


def _gla_kernel(q_ref, k_ref, v_ref, r_ref, misc_ref, wup_ref, bup_ref, ng_ref, o_ref, st_ref, *, nchunks):
    @pl.when(pl.program_id(2) == 0)
    def _():
        st_ref[...] = jnp.zeros_like(st_ref)

    ng = ng_ref[...]
    wup = wup_ref[0]
    bup = bup_ref[0]

    def body(c, carry):
        sl = pl.ds(pl.multiple_of(c * CHUNK, CHUNK), CHUNK)
        z = _dot(misc_ref[sl, :].astype(BF16), wup) + bup
        la = (jnp.minimum(z, 0.0) - jnp.log(1.0 + jnp.exp(-jnp.abs(z)))) * (1.0 / C_TAU)
        o, st = _diag_gated_chunk(q_ref[sl, :] * (HEAD_DK ** -0.5), k_ref[sl, :], v_ref[sl, :], la,
                                  st_ref[...])
        st_ref[...] = st
        o_ref[sl, :] = _rms_gate(o, ng, r_ref[sl, :]).astype(o_ref.dtype)
        return carry

    lax.fori_loop(0, nchunks, body, 0)


def _gla(proj, w_up_pad, b_up, norm_g, bsz, seq, tblk):
    nt = seq // tblk

    def col(base):
        return pl.BlockSpec((tblk, LANES), lambda b, h, t: (b * nt + t, base + h))

    def col2(base):
        return pl.BlockSpec((tblk, C_DV), lambda b, h, t: (b * nt + t, base // 2 + h))

    return pl.pallas_call(
        functools.partial(_gla_kernel, nchunks=tblk // CHUNK),
        grid=(bsz, C_HEADS, nt),
        in_specs=[col(COL_CQ), col(COL_CK), col2(COL_CV), col2(COL_CR),
                  pl.BlockSpec((tblk, LANES), lambda b, h, t: (b * nt + t, COL_MISC)),
                  pl.BlockSpec((1, LANES, LANES), lambda b, h, t: (h, 0, 0)),
                  pl.BlockSpec((1, 1, LANES), lambda b, h, t: (h, 0, 0)),
                  pl.BlockSpec((1, C_DV), lambda b, h, t: (0, 0))],
        out_specs=pl.BlockSpec((tblk, C_DV), lambda b, h, t: (b * nt + t, h)),
        out_shape=jax.ShapeDtypeStruct((bsz * seq, C_HEADS * C_DV), BF16),
        scratch_shapes=[pltpu.VMEM((C_DV, HEAD_DK), F32)],
        compiler_params=_cparams(("arbitrary", "arbitrary", "arbitrary")),
        name="gla",
    )(proj, proj, proj, proj, proj, w_up_pad, b_up, norm_g.reshape(1, C_DV))


def _dot3(a, b):
    ah, al = _split(a)
    bh, bl = _split(b)
    return _dot(ah, bh) + _dot(ah, bl) + _dot(al, bh)


def _unit_lower_inverse(m):
    eye = jnp.where(_iota((CHUNK, CHUNK), 0) == _iota((CHUNK, CHUNK), 1), 1.0, 0.0)
    p = -m
    t = eye + p
    for _ in range(5):
        p = _dot3(p, p)
        t = t + _dot3(t, p)
    return t


def _gdn_chunk(q, k, v, beta, g, state):
    incl = _tri_incl()
    strict = _iota((CHUNK, CHUNK), 0) > _iota((CHUNK, CHUNK), 1)
    tri = jnp.where(incl, 1.0, 0.0).astype(BF16)
    gc = _dot_exact_lhs(tri, g)
    upper = jnp.where(strict, 1.0, 0.0)
    dmat = _dot_exact_lhs(tri, g[:, :CHUNK] * upper)
    gamma = jnp.where(incl, jnp.exp(jnp.where(incl, dmat, 0.0)), 0.0)
    qb, kb = q.astype(BF16), k.astype(BF16)
    kk = _dot_nt(kb, kb)
    m = jnp.where(strict, beta[:, :CHUNK] * kk * gamma, 0.0)
    t = _unit_lower_inverse(m)
    egc = jnp.exp(gc)
    rhs = jnp.concatenate([v * beta, k * (beta * egc)], axis=1).astype(BF16)
    th, tl = _split(t)
    sol = _dot(th, rhs) + _dot(tl, rhs)
    u, w = sol[:, :B_DV], sol[:, B_DV:]
    attn = jnp.where(incl, _dot_nt(qb, kb) * gamma, 0.0)
    gc_last = gc[CHUNK - 1:CHUNK, :]
    q_dec = (q * egc).astype(BF16)
    k_dec = (k * jnp.exp(gc_last - gc)).astype(BF16)
    st_b = state.astype(BF16)
    v_new = u - _dot(w.astype(BF16), st_b)
    o = _dot(q_dec, st_b) + _dot(attn.astype(BF16), v_new.astype(BF16))
    new_state = state * jnp.exp(gc_last) + _dot_tn(k_dec, v_new.astype(BF16))
    return o, new_state


def _l2n(x):
    return x * lax.rsqrt(jnp.sum(x * x, axis=-1, keepdims=True) + RMS_EPS)


def _gdn_kernel(q_ref, k_ref, v_ref, z_ref, misc_ref, cw_ref, al_ref, dt_ref, ng_ref, o_ref,
                st_ref, buf_ref, *, nchunks, tblk):
    h = pl.program_id(1)

    @pl.when(pl.program_id(2) == 0)
    def _():
        st_ref[...] = jnp.zeros_like(st_ref)
        buf_ref[...] = jnp.zeros_like(buf_ref)

    conv = []
    for n, ref in enumerate((q_ref, k_ref, v_ref)):
        buf_ref[n, 8:, :] = ref[...]
        acc = jnp.zeros((tblk, LANES), F32)
        for j in range(B_CONV):
            acc = acc + cw_ref[n, j:j + 1, :] * buf_ref[n, 8 - (B_CONV - 1) + j:8 - (B_CONV - 1) + j + tblk, :]
        buf_ref[n, 0:8, :] = buf_ref[n, tblk:tblk + 8, :]
        conv.append(_silu(acc))
    buf_ref[3, 0:tblk, :] = _l2n(conv[0]) * (HEAD_DK ** -0.5)
    buf_ref[4, 0:tblk, :] = _l2n(conv[1])
    buf_ref[5, 0:tblk, :] = conv[2]

    lane_r = _iota((LANES, LANES), 0)
    sel_beta = jnp.where(lane_r == MISC_BETA + h, 1.0, 0.0).astype(BF16)
    sel_dec = jnp.where(lane_r == MISC_DECAY + h, 1.0, 0.0).astype(BF16)
    mh, ml = _split(misc_ref[...])
    beta_logit = _dot(mh, sel_beta) + _dot(ml, sel_beta)
    dec_in = _dot(mh, sel_dec) + _dot(ml, sel_dec) + dt_ref[...]
    softplus = jnp.maximum(dec_in, 0.0) + jnp.log(1.0 + jnp.exp(-jnp.abs(dec_in)))
    buf_ref[6, 0:tblk, :] = _sigmoid(beta_logit)
    buf_ref[7, 0:tblk, :] = -jnp.exp(al_ref[...]) * softplus

    ng = ng_ref[...]

    def body(c, carry):
        sl = pl.ds(pl.multiple_of(c * CHUNK, CHUNK), CHUNK)
        o, st = _gdn_chunk(buf_ref[3, sl, :], buf_ref[4, sl, :], buf_ref[5, sl, :], buf_ref[6, sl, :],
                           buf_ref[7, sl, :], st_ref[...])
        st_ref[...] = st
        o_ref[sl, :] = _rms_gate(o, ng, z_ref[sl, :]).astype(o_ref.dtype)
        return carry

    lax.fori_loop(0, nchunks, body, 0)


def _gdn(proj, conv_w, a_log, dt_bias, norm_g, bsz, seq, tblk):
    nt = seq // tblk

    def col(base):
        return pl.BlockSpec((tblk, LANES), lambda b, h, t: (b * nt + t, base + h))

    def per_head(shape):
        return pl.BlockSpec((None,) + shape, lambda b, h, t: (h,) + (0,) * len(shape))

    return pl.pallas_call(
        functools.partial(_gdn_kernel, nchunks=tblk // CHUNK, tblk=tblk),
        grid=(bsz, B_HEADS, nt),
        in_specs=[col(COL_BQ), col(COL_BK), col(COL_BV), col(COL_BZ),
                  pl.BlockSpec((tblk, LANES), lambda b, h, t: (b * nt + t, COL_MISC)),
                  per_head((3, B_CONV, LANES)), per_head((1, LANES)), per_head((1, LANES)),
                  pl.BlockSpec((1, B_DV), lambda b, h, t: (0, 0))],
        out_specs=pl.BlockSpec((tblk, B_DV), lambda b, h, t: (b * nt + t, h)),
        out_shape=jax.ShapeDtypeStruct((bsz * seq, B_HEADS * B_DV), BF16),
        scratch_shapes=[pltpu.VMEM((HEAD_DK, B_DV), F32), pltpu.VMEM((8, tblk + 8, LANES), F32)],
        compiler_params=_cparams(("arbitrary", "arbitrary", "arbitrary")),
        name="gdn",
    )(proj, proj, proj, proj, proj, conv_w, a_log, dt_bias, norm_g.reshape(1, B_DV))


def _xattn_kernel(xb_ref, xf_ref, wq_ref, kv_ref, wo_ref, g_ref, b_ref, of_ref, ob_ref, *, alpha):
    q = _dot(xb_ref[...], wq_ref[...])
    width = X_HEADS * X_HEAD_DIM
    outs = []
    for hd in range(X_HEADS):
        lo = hd * X_HEAD_DIM
        qh = q[:, lo:lo + X_HEAD_DIM].astype(BF16)
        kh = kv_ref[:, lo:lo + X_HEAD_DIM]
        vh = kv_ref[:, width + lo:width + lo + X_HEAD_DIM]
        s = _dot_nt(qh, kh) * (X_HEAD_DIM ** -0.5)
        s = s - jnp.max(s, axis=-1, keepdims=True)
        p = jnp.exp(s)
        p = p / jnp.sum(p, axis=-1, keepdims=True)
        outs.append(_dot(p.astype(BF16), vh))
    o = jnp.concatenate(outs, axis=1).astype(BF16)
    y = _ln_rows(alpha * xf_ref[...] + _dot(o, wo_ref[...]), g_ref[...], b_ref[...])
    of_ref[...] = y
    ob_ref[...] = y.astype(BF16)


def _xattn(xb, xf, wq, kv, wo, g, b, alpha, bsz, seq, tq):
    d = xf.shape[1]
    nt = seq // tq
    width = X_HEADS * X_HEAD_DIM
    n_mem = kv.shape[1]
    return pl.pallas_call(
        functools.partial(_xattn_kernel, alpha=alpha),
        grid=(bsz, nt),
        in_specs=[pl.BlockSpec((tq, d), lambda bb, t: (bb * nt + t, 0)),
                  pl.BlockSpec((tq, d), lambda bb, t: (bb * nt + t, 0)),
                  pl.BlockSpec((d, width), lambda bb, t: (0, 0)),
                  pl.BlockSpec((None, n_mem, 2 * width), lambda bb, t: (bb, 0, 0)),
                  pl.BlockSpec((width, d), lambda bb, t: (0, 0)),
                  pl.BlockSpec((1, d), lambda bb, t: (0, 0)),
                  pl.BlockSpec((1, d), lambda bb, t: (0, 0))],
        out_specs=[pl.BlockSpec((tq, d), lambda bb, t: (bb * nt + t, 0)),
                   pl.BlockSpec((tq, d), lambda bb, t: (bb * nt + t, 0))],
        out_shape=[jax.ShapeDtypeStruct((bsz * seq, d), F32), jax.ShapeDtypeStruct((bsz * seq, d), BF16)],
        compiler_params=_cparams(("arbitrary", "arbitrary")),
        name="xattn",
    )(xb, xf, wq, kv, wo, g.reshape(1, d), b.reshape(1, d))


def _router_kernel(xb_ref, w_ref, b_ref, o_ref):
    logits = _dot(xb_ref[...], w_ref[...]) + b_ref[...]
    lane = _iota(logits.shape, 1).astype(F32)
    neg = jnp.float32(-3.0e38)
    picked = []
    vals = []
    work = logits
    for _ in range(TOP_K):
        mx = jnp.max(work, axis=-1, keepdims=True)
        idx = jnp.min(jnp.where(work == mx, lane, float(LANES)), axis=-1, keepdims=True)
        sel = lane == idx
        picked.append(sel)
        vals.append(mx)
        work = jnp.where(sel, neg, work)
    es = [jnp.exp(vv - vals[0]) for vv in vals]
    inv = 1.0 / (es[0] + es[1] + es[2] + es[3])
    comb = jnp.zeros_like(logits)
    for sel, e in zip(picked, es):
        comb = comb + jnp.where(sel, e * inv, 0.0)
    o_ref[...] = comb


def _router(xb, w_pad, b_pad, tm):
    m, d = xb.shape
    return pl.pallas_call(
        _router_kernel,
        grid=(m // tm,),
        in_specs=[pl.BlockSpec((tm, d), lambda i: (i, 0)),
                  pl.BlockSpec((d, LANES), lambda i: (0, 0)),
                  pl.BlockSpec((1, LANES), lambda i: (0, 0))],
        out_specs=pl.BlockSpec((tm, LANES), lambda i: (i, 0)),
        out_shape=jax.ShapeDtypeStruct((m, LANES), F32),
        compiler_params=_cparams(("arbitrary",)),
        name="router",
    )(xb, w_pad, b_pad)


def _moe_kernel(xb_ref, xf_ref, comb_ref, wg_ref, bg_ref, wu_ref, bu_ref, wd_ref, bd_ref, g_ref, b_ref,
                of_ref, ob_ref, *, alpha, ne):
    e = pl.program_id(1)
    xb = xb_ref[...]
    gate = jnp.minimum(_dot(xb, wg_ref[...]) + bg_ref[...], SWIGLU_LIMIT)
    lin = jnp.clip(_dot(xb, wu_ref[...]) + bu_ref[...], -SWIGLU_LIMIT, SWIGLU_LIMIT)
    hid = (lin + 1.0) * gate * _sigmoid(SWIGLU_ALPHA * gate)
    sel = jnp.where(_iota((LANES, EXPERT_FF), 0) == e, 1.0, 0.0).astype(BF16)
    ch, cl = _split(comb_ref[...])
    weight = _dot(ch, sel) + _dot(cl, sel)
    part = _dot((hid * weight).astype(BF16), wd_ref[...])

    @pl.when(e == 0)
    def _():
        of_ref[...] = part

    @pl.when(e > 0)
    def _():
        of_ref[...] += part

    @pl.when(e == ne - 1)
    def _():
        y = of_ref[...] + _dot(comb_ref[...].astype(BF16), bd_ref[...])
        y = _ln_rows(alpha * xf_ref[...] + y, g_ref[...], b_ref[...])
        of_ref[...] = y
        ob_ref[...] = y.astype(BF16)


def _moe(xb, xf, comb, wg, bg, wu, bu, wd, bd_pad, g, b, alpha, tm):
    m, d = xf.shape
    ne, _, ff = wg.shape
    return pl.pallas_call(
        functools.partial(_moe_kernel, alpha=alpha, ne=ne),
        grid=(m // tm, ne),
        in_specs=[pl.BlockSpec((tm, d), lambda i, e: (i, 0)),
                  pl.BlockSpec((tm, d), lambda i, e: (i, 0)),
                  pl.BlockSpec((tm, LANES), lambda i, e: (i, 0)),
                  pl.BlockSpec((None, d, ff), lambda i, e: (e, 0, 0)),
                  pl.BlockSpec((None, 1, ff), lambda i, e: (e, 0, 0)),
                  pl.BlockSpec((None, d, ff), lambda i, e: (e, 0, 0)),
                  pl.BlockSpec((None, 1, ff), lambda i, e: (e, 0, 0)),
                  pl.BlockSpec((None, ff, d), lambda i, e: (e, 0, 0)),
                  pl.BlockSpec((LANES, d), lambda i, e: (0, 0)),
                  pl.BlockSpec((1, d), lambda i, e: (0, 0)),
                  pl.BlockSpec((1, d), lambda i, e: (0, 0))],
        out_specs=[pl.BlockSpec((tm, d), lambda i, e: (i, 0)),
                   pl.BlockSpec((tm, d), lambda i, e: (i, 0))],
        out_shape=[jax.ShapeDtypeStruct((m, d), F32), jax.ShapeDtypeStruct((m, d), BF16)],
        compiler_params=_cparams(("arbitrary", "arbitrary")),
        name="moe",
    )(xb, xf, comb, wg, bg, wu, bu, wd, bd_pad, g.reshape(1, d), b.reshape(1, d))


def _pick(n, prefs):
    for p in prefs:
        if n % p == 0:
            return p
    raise ValueError(f"no tile in {prefs} divides {n}")


def kernel(x, mem, w_in, hgrn_lb_raw, hgrn_norm_g, gdn_conv_w, gdn_a_log, gdn_dt_bias, gdn_norm_g, gla_w_up, gla_b_up, gla_norm_g, w_out, ln_mix_g, ln_mix_b, mem_ln_g, mem_ln_b, xattn_wq, xattn_wkv, xattn_wo, ln_xattn_g, ln_xattn_b, w_router, b_router, w_gate, b_gate, w_up, b_up, w_down, b_down, ln_ffn_g, ln_ffn_b):
    bsz, seq, d = x.shape
    depth = w_in.shape[0]
    n_mem = mem.shape[1]
    ntok = bsz * seq
    alpha = float((2 * depth) ** 0.25)
    assert seq % CHUNK == 0 and d % LANES == 0

    tblk = _pick(seq, (512, 256, 128, 64))
    tm_big = _pick(ntok, (512, 256, 128, 64))
    tm_moe = _pick(ntok, (256, 128, 64))

    a_w, b_w, c_w = A_HEADS * HEAD_DK, B_HEADS * HEAD_DK, C_HEADS * HEAD_DK
    o_bb = 4 * a_w + (2 * b_w + B_HEADS * B_DV) + B_HEADS * B_DV
    o_cq = o_bb + 2 * B_HEADS
    o_clr = o_cq + 2 * c_w + 2 * C_HEADS * C_DV
    n_cols = o_clr + C_RANK
    assert w_in.shape[2] == n_cols
    main_cols = o_bb + (o_clr - o_cq)
    assert main_cols == COL_MISC * LANES
    w_in_p = jnp.concatenate(
        [w_in[:, :, :o_bb], w_in[:, :, o_cq:o_clr], w_in[:, :, o_bb:o_cq], w_in[:, :, o_clr:],
         jnp.zeros((depth, d, IN_COLS_PAD - n_cols), w_in.dtype)], axis=2).astype(BF16)
    w_out_b = w_out.astype(BF16)
    wq_b, wkv_b, wo_b = xattn_wq.astype(BF16), xattn_wkv.astype(BF16), xattn_wo.astype(BF16)
    wr_p = jnp.concatenate([w_router, jnp.zeros((depth, d, LANES - N_EXPERTS), F32)], axis=2).astype(BF16)
    br_p = jnp.concatenate([b_router, jnp.full((depth, LANES - N_EXPERTS), -1.0e30, F32)], axis=1)
    wg_b, wu_b, wd_b = w_gate.astype(BF16), w_up.astype(BF16), w_down.astype(BF16)
    bd_p = jnp.concatenate([b_down, jnp.zeros((depth, LANES - N_EXPERTS, d), F32)], axis=1).astype(BF16)
    gla_wup_p = jnp.zeros((depth, LANES, c_w), F32).at[:, MISC_CODE:MISC_CODE + C_RANK, :].set(gla_w_up)
    gla_wup_p = gla_wup_p.reshape(depth, LANES, C_HEADS, HEAD_DK).transpose(0, 2, 1, 3).astype(BF16)
    gla_bup_r = gla_b_up.reshape(depth, C_HEADS, 1, HEAD_DK)
    conv_r = gdn_conv_w.reshape(depth, B_CONV, 3, B_HEADS, HEAD_DK).transpose(0, 3, 2, 1, 4)
    a_log_r = jnp.broadcast_to(gdn_a_log[:, :, None, None], (depth, B_HEADS, 1, LANES))
    dt_r = jnp.broadcast_to(gdn_dt_bias[:, :, None, None], (depth, B_HEADS, 1, LANES))
    lb_all = jnp.cumsum(jax.nn.softmax(hgrn_lb_raw.astype(F32), axis=0), axis=0)
    lb_r = (lb_all - lb_all[0:1]).reshape(depth, A_HEADS, 1, HEAD_DK)

    mem_n = _layer_norm(mem.reshape(bsz * n_mem, d), mem_ln_g, mem_ln_b, BF16, _pick(bsz * n_mem, (256, 128, 64, 8)))

    xf = x.reshape(ntok, d)
    xb = xf.astype(BF16)
    for l in range(depth):
        proj = _matmul(xb, w_in_p[l], F32, tm_big, 1536)
        oa = _hgrn(proj, lb_r[l], hgrn_norm_g[l], bsz, seq, tblk)
        ob = _gdn(proj, conv_r[l], a_log_r[l], dt_r[l], gdn_norm_g[l], bsz, seq, tblk)
        oc = _gla(proj, gla_wup_p[l], gla_bup_r[l], gla_norm_g[l], bsz, seq, tblk)
        o = jnp.concatenate([oa, ob, oc], axis=1)
        xf, xb = _proj_ln(o, w_out_b[l], xf, ln_mix_g[l], ln_mix_b[l], alpha, tm_moe, 512)

        kv = _matmul(mem_n, wkv_b[l], BF16, _pick(bsz * n_mem, (256, 128, 64, 8)), 2 * X_HEADS * X_HEAD_DIM)
        xf, xb = _xattn(xb, xf, wq_b[l], kv.reshape(bsz, n_mem, -1), wo_b[l], ln_xattn_g[l], ln_xattn_b[l],
                        alpha, bsz, seq, _pick(seq, (256, 128, 64)))

        comb = _router(xb, wr_p[l], br_p[l, None, :], tm_big)
        xf, xb = _moe(xb, xf, comb, wg_b[l], b_gate[l][:, None, :], wu_b[l], b_up[l][:, None, :], wd_b[l],
                      bd_p[l], ln_ffn_g[l], ln_ffn_b[l], alpha, tm_moe)
    return xf.reshape(bsz, seq, d)
```

```python
import functools

import numpy as np
import jax
import jax.numpy as jnp
from jax import lax
from jax.experimental import pallas as pl
from jax.experimental.pallas import tpu as pltpu

F32 = jnp.float32
BF16 = jnp.bfloat16

LANES = 128
CHUNK = 64
VMEM_LIMIT = 60 * 1024 * 1024
ROW_SLICE = 256
COL_SLICE = 1024

A_HEADS, B_HEADS, C_HEADS = 8, 12, 6
HEAD_DK = 128
A_DV, B_DV, C_DV = 128, 128, 256
B_CONV = 4
C_RANK = 16
C_TAU = 16.0
X_HEADS, X_HEAD_DIM = 4, 128
N_EXPERTS, TOP_K, EXPERT_FF = 32, 4, 256
SWIGLU_LIMIT, SWIGLU_ALPHA = 7.0, 1.702
LN_EPS, RMS_EPS = 1e-5, 1e-6

COL_AQ, COL_AF, COL_AI, COL_AG = 0, 8, 16, 24
COL_BQ, COL_BK, COL_BV, COL_BZ = 32, 44, 56, 68
COL_CQ, COL_CK, COL_CV, COL_CR = 80, 86, 92, 104
COL_MISC = 116
MISC_BETA, MISC_DECAY, MISC_CODE = 0, 12, 24
IN_COLS_PAD = 120 * LANES


def _cparams(sem):
    return pltpu.CompilerParams(dimension_semantics=sem, vmem_limit_bytes=VMEM_LIMIT)


def _dot(a, b):
    return jnp.dot(a, b, preferred_element_type=F32)


def _dot_nt(a, b):
    return lax.dot_general(a, b, (((1,), (1,)), ((), ())), preferred_element_type=F32)


def _dot_tn(a, b):
    return lax.dot_general(a, b, (((0,), (0,)), ((), ())), preferred_element_type=F32)


def _split(x):
    hi = x.astype(BF16)
    lo = (x - hi.astype(F32)).astype(BF16)
    return hi, lo


def _dot_exact_lhs(sel, x):
    hi, lo = _split(x)
    return _dot(sel, hi) + _dot(sel, lo)


def _sigmoid(x):
    return 1.0 / (1.0 + jnp.exp(-x))


def _silu(x):
    return x * _sigmoid(x)


def _iota(shape, dim):
    return lax.broadcasted_iota(jnp.int32, shape, dim)


def _full_spec(shape):
    return pl.BlockSpec(shape, lambda *_: (0,) * len(shape))


def _mm_kernel(a_ref, w_ref, o_ref):
    o_ref[...] = _dot(a_ref[...], w_ref[...]).astype(o_ref.dtype)


def _matmul(a, w, layer, out_dtype, tm, tn):
    m, k = a.shape
    n = w.shape[2]
    return pl.pallas_call(
        _mm_kernel,
        grid=(n // tn, m // tm),
        in_specs=[pl.BlockSpec((tm, k), lambda j, i: (i, 0)),
                  pl.BlockSpec((None, k, tn), lambda j, i: (layer, 0, j))],
        out_specs=pl.BlockSpec((tm, tn), lambda j, i: (i, j)),
        out_shape=jax.ShapeDtypeStruct((m, n), out_dtype),
        compiler_params=_cparams(("arbitrary", "arbitrary")),
        name="matmul",
    )(a, w)


def _ln_rows(y, g, b):
    mu = jnp.mean(y, axis=-1, keepdims=True)
    d = y - mu
    var = jnp.mean(d * d, axis=-1, keepdims=True)
    return d * lax.rsqrt(var + LN_EPS) * g + b


def _ln_kernel(x_ref, g_ref, b_ref, o_ref):
    o_ref[...] = _ln_rows(x_ref[...], g_ref[...], b_ref[...]).astype(o_ref.dtype)


def _layer_norm(x, g, b, out_dtype, tm):
    m, d = x.shape
    return pl.pallas_call(
        _ln_kernel,
        grid=(m // tm,),
        in_specs=[pl.BlockSpec((tm, d), lambda i: (i, 0)),
                  pl.BlockSpec((1, d), lambda i: (0, 0)),
                  pl.BlockSpec((1, d), lambda i: (0, 0))],
        out_specs=pl.BlockSpec((tm, d), lambda i: (i, 0)),
        out_shape=jax.ShapeDtypeStruct((m, d), out_dtype),
        compiler_params=_cparams(("arbitrary",)),
        name="layer_norm",
    )(x, g.reshape(1, d), b.reshape(1, d))


def _proj_ln_kernel(a0_ref, a1_ref, a2_ref, w_ref, x_ref, g_ref, b_ref, of_ref, ob_ref, *, alpha, bounds):
    kk = pl.program_id(1)
    n0, n1, nk = bounds
    tm, d = of_ref.shape
    rows = [slice(r, r + ROW_SLICE) for r in range(0, tm, ROW_SLICE)]
    cols = [slice(c, c + COL_SLICE) for c in range(0, d, COL_SLICE)]

    @pl.when(kk == 0)
    def _():
        of_ref[...] = jnp.zeros_like(of_ref)

    def accumulate(a_ref):
        a = a_ref[...]
        for sl in cols:
            of_ref[:, sl] += _dot(a, w_ref[:, sl])

    pl.when(kk < n0)(lambda: accumulate(a0_ref))
    pl.when((kk >= n0) & (kk < n1))(lambda: accumulate(a1_ref))
    pl.when(kk >= n1)(lambda: accumulate(a2_ref))

    @pl.when(kk == nk - 1)
    def _():
        for sl in rows:
            y = _ln_rows(alpha * x_ref[sl, :] + of_ref[sl, :], g_ref[...], b_ref[...])
            of_ref[sl, :] = y
            ob_ref[sl, :] = y.astype(BF16)


def _proj_ln(a0, a1, a2, w, layer, x, g, b, alpha, tm, tk):
    m = a0.shape[0]
    d = w.shape[2]
    n0 = a0.shape[1] // tk
    n1 = n0 + a1.shape[1] // tk
    nk = n1 + a2.shape[1] // tk
    assert nk * tk == w.shape[1]
    return pl.pallas_call(
        functools.partial(_proj_ln_kernel, alpha=alpha, bounds=(n0, n1, nk)),
        grid=(m // tm, nk),
        in_specs=[pl.BlockSpec((tm, tk), lambda i, kk: (i, jnp.minimum(kk, n0 - 1))),
                  pl.BlockSpec((tm, tk), lambda i, kk: (i, jnp.clip(kk - n0, 0, n1 - n0 - 1))),
                  pl.BlockSpec((tm, tk), lambda i, kk: (i, jnp.clip(kk - n1, 0, nk - n1 - 1))),
                  pl.BlockSpec((None, tk, d), lambda i, kk: (layer, kk, 0)),
                  pl.BlockSpec((tm, d), lambda i, kk: (i, 0)),
                  pl.BlockSpec((1, d), lambda i, kk: (0, 0)),
                  pl.BlockSpec((1, d), lambda i, kk: (0, 0))],
        out_specs=[pl.BlockSpec((tm, d), lambda i, kk: (i, 0)),
                   pl.BlockSpec((tm, d), lambda i, kk: (i, 0))],
        out_shape=[jax.ShapeDtypeStruct((m, d), F32), jax.ShapeDtypeStruct((m, d), BF16)],
        compiler_params=_cparams(("arbitrary", "arbitrary")),
        name="proj_ln",
    )(a0, a1, a2, w, x, g.reshape(1, d), b.reshape(1, d))


_LEVELS = (32, 16, 8, 4, 2, 1)
_N_SEL = len(_LEVELS) + 2


def _range_selectors():
    i = np.arange(CHUNK)[:, None]
    t = np.arange(CHUNK)[None, :]
    blocks = [t <= i]
    for s in _LEVELS:
        m = (i & ~(2 * s - 1)) | (s - 1)
        later = (i & s) != 0
        blocks.append(np.where(later, (t > m) & (t <= i), (t > i) & (t <= m)))
    blocks.append(t > i)
    return jnp.asarray(np.concatenate(blocks, axis=0).astype(np.float32), dtype=BF16)


def _level_masks():
    i = np.arange(CHUNK)[:, None]
    j = np.arange(CHUNK)[None, :]
    masks = [i == j]
    for s in _LEVELS:
        sh = s.bit_length()
        masks.append(((i >> sh) == (j >> sh)) & ((i & s) != 0) & ((j & s) == 0))
    return jnp.asarray(np.stack(masks).astype(np.float32))


def _diag_gated_block(q_ref, k_ref, v_ref, la_ref, e_ref, rsel_ref, mask_ref, state_t, nchunks, group, emit):
    dk = q_ref.shape[1]
    row = _iota((CHUNK, dk), 0)
    laters = [(row & s) != 0 for s in _LEVELS]
    rsel = rsel_ref[...]
    span = _N_SEL * CHUNK
    for g0 in range(0, nchunks, group):
        cs = list(range(g0, min(g0 + group, nchunks)))
        for c in cs:
            hi, lo = _split(la_ref[c * CHUNK:(c + 1) * CHUNK, :])
            e_ref[c * span:(c + 1) * span, :] = _dot(rsel, hi) + _dot(rsel, lo)
        qs = [q_ref[c * CHUNK:(c + 1) * CHUNK, :] for c in cs]
        ks = [k_ref[c * CHUNK:(c + 1) * CHUNK, :] for c in cs]
        vbs = [v_ref[c * CHUNK:(c + 1) * CHUNK, :].astype(BF16) for c in cs]
        scores = [_dot_nt(q.astype(BF16), k.astype(BF16)) * mask_ref[0] for q, k in zip(qs, ks)]
        for n in range(len(_LEVELS)):
            zs = [(jnp.where(laters[n], q, k)
                   * jnp.exp(e_ref[c * span + (n + 1) * CHUNK:c * span + (n + 2) * CHUNK, :])).astype(BF16)
                  for c, q, k in zip(cs, qs, ks)]
            scores = [sc + _dot_nt(z, z) * mask_ref[n + 1] for sc, z in zip(scores, zs)]
        o_intra = [_dot(sc.astype(BF16), vb) for sc, vb in zip(scores, vbs)]
        q_dec = [(q * jnp.exp(e_ref[c * span:c * span + CHUNK, :])).astype(BF16) for c, q in zip(cs, qs)]
        k_dec = [(k * jnp.exp(e_ref[c * span + 7 * CHUNK:(c + 1) * span, :])).astype(BF16) for c, k in zip(cs, ks)]
        ds_t = [_dot_tn(vb, kd) for vb, kd in zip(vbs, k_dec)]
        for n, c in enumerate(cs):
            emit(c, o_intra[n] + _dot_nt(q_dec[n], state_t.astype(BF16)))
            decay = jnp.exp(e_ref[c * span + CHUNK - 1:c * span + CHUNK, :])
            state_t = state_t * decay + ds_t[n]
    return state_t


def _rms_gate(o, g, gate):
    y = o * lax.rsqrt(jnp.mean(o * o, axis=-1, keepdims=True) + RMS_EPS)
    return y * g * _silu(gate)


def _hgrn_kernel(q_ref, f_ref, i_ref, g_ref, lb_ref, ng_ref, rsel_ref, mask_ref, o_ref,
                 st_ref, qs_ref, ks_ref, la_ref, e_ref, *, nchunks, group):
    @pl.when(pl.program_id(2) == 0)
    def _():
        st_ref[...] = jnp.zeros_like(st_ref)

    lb = lb_ref[0]
    f = lb + (1.0 - lb) * _sigmoid(f_ref[...])
    qs_ref[...] = _silu(q_ref[...])
    ks_ref[...] = 1.0 - f
    la_ref[...] = jnp.log(f)
    ng = ng_ref[...]

    def emit(c, o):
        sl = slice(c * CHUNK, (c + 1) * CHUNK)
        o_ref[sl, :] = _rms_gate(o, ng, g_ref[sl, :]).astype(o_ref.dtype)

    st_ref[...] = _diag_gated_block(qs_ref, ks_ref, i_ref, la_ref, e_ref, rsel_ref, mask_ref, st_ref[...],
                                    nchunks, group, emit)


def _hgrn(proj, lb, norm_g, rsel, masks, bsz, seq, tblk, group):
    nt = seq // tblk

    def col(base):
        return pl.BlockSpec((tblk, LANES), lambda b, h, t: (b * nt + t, base + h))

    return pl.pallas_call(
        functools.partial(_hgrn_kernel, nchunks=tblk // CHUNK, group=group),
        grid=(bsz, A_HEADS, nt),
        in_specs=[col(COL_AQ), col(COL_AF), col(COL_AI), col(COL_AG),
                  pl.BlockSpec((1, 1, LANES), lambda b, h, t: (h, 0, 0)),
                  _full_spec((1, A_DV)), _full_spec(rsel.shape), _full_spec(masks.shape)],
        out_specs=pl.BlockSpec((tblk, A_DV), lambda b, h, t: (b * nt + t, h)),
        out_shape=jax.ShapeDtypeStruct((bsz * seq, A_HEADS * A_DV), BF16),
        scratch_shapes=[pltpu.VMEM((A_DV, HEAD_DK), F32), pltpu.VMEM((tblk, HEAD_DK), F32),
                        pltpu.VMEM((tblk, HEAD_DK), F32), pltpu.VMEM((tblk, HEAD_DK), F32),
                        pltpu.VMEM((tblk * _N_SEL, HEAD_DK), F32)],
        compiler_params=_cparams(("arbitrary", "arbitrary", "arbitrary")),
        name="hgrn2",
    )(proj, proj, proj, proj, lb, norm_g.reshape(1, A_DV), rsel, masks)


def _gla_kernel(q_ref, k_ref, v_ref, r_ref, misc_ref, wup_ref, bup_ref, ng_ref, rsel_ref, mask_ref, o_ref,
                st_ref, qs_ref, la_ref, e_ref, *, nchunks, group):
    @pl.when(pl.program_id(2) == 0)
    def _():
        st_ref[...] = jnp.zeros_like(st_ref)

    z = _dot(misc_ref[...].astype(BF16), wup_ref[0]) + bup_ref[0]
    la_ref[...] = (jnp.minimum(z, 0.0) - jnp.log(1.0 + jnp.exp(-jnp.abs(z)))) * (1.0 / C_TAU)
    qs_ref[...] = q_ref[...] * (HEAD_DK ** -0.5)
    ng = ng_ref[...]

    def emit(c, o):
        sl = slice(c * CHUNK, (c + 1) * CHUNK)
        o_ref[sl, :] = _rms_gate(o, ng, r_ref[sl, :]).astype(o_ref.dtype)

    st_ref[...] = _diag_gated_block(qs_ref, k_ref, v_ref, la_ref, e_ref, rsel_ref, mask_ref, st_ref[...],
                                    nchunks, group, emit)


def _gla(proj, w_up_pad, b_up, norm_g, rsel, masks, bsz, seq, tblk, group):
    nt = seq // tblk

    def col(base):
        return pl.BlockSpec((tblk, LANES), lambda b, h, t: (b * nt + t, base + h))

    def col2(base):
        return pl.BlockSpec((tblk, C_DV), lambda b, h, t: (b * nt + t, base // 2 + h))

    return pl.pallas_call(
        functools.partial(_gla_kernel, nchunks=tblk // CHUNK, group=group),
        grid=(bsz, C_HEADS, nt),
        in_specs=[col(COL_CQ), col(COL_CK), col2(COL_CV), col2(COL_CR),
                  pl.BlockSpec((tblk, LANES), lambda b, h, t: (b * nt + t, COL_MISC)),
                  pl.BlockSpec((1, LANES, LANES), lambda b, h, t: (h, 0, 0)),
                  pl.BlockSpec((1, 1, LANES), lambda b, h, t: (h, 0, 0)),
                  _full_spec((1, C_DV)), _full_spec(rsel.shape), _full_spec(masks.shape)],
        out_specs=pl.BlockSpec((tblk, C_DV), lambda b, h, t: (b * nt + t, h)),
        out_shape=jax.ShapeDtypeStruct((bsz * seq, C_HEADS * C_DV), BF16),
        scratch_shapes=[pltpu.VMEM((C_DV, HEAD_DK), F32), pltpu.VMEM((tblk, HEAD_DK), F32),
                        pltpu.VMEM((tblk, HEAD_DK), F32), pltpu.VMEM((tblk * _N_SEL, HEAD_DK), F32)],
        compiler_params=_cparams(("arbitrary", "arbitrary", "arbitrary")),
        name="gla",
    )(proj, proj, proj, proj, proj, w_up_pad, b_up, norm_g.reshape(1, C_DV), rsel, masks)


def _dot3(a, b):
    ah, al = _split(a)
    bh, bl = _split(b)
    return _dot(ah, bh) + _dot(ah, bl) + _dot(al, bh)


def _gdn_block(buf_ref, state, nchunks, group, emit):
    ri = _iota((CHUNK, CHUNK), 0)
    ci = _iota((CHUNK, CHUNK), 1)
    incl = ri >= ci
    strict = ri > ci
    eye = jnp.where(ri == ci, 1.0, 0.0)
    tri = jnp.where(incl, 1.0, 0.0).astype(BF16)
    upper = jnp.where(strict, 1.0, 0.0)
    for g0 in range(0, nchunks, group):
        cs = list(range(g0, min(g0 + group, nchunks)))
        sls = [slice(c * CHUNK, (c + 1) * CHUNK) for c in cs]
        qs = [buf_ref[3, sl, :] for sl in sls]
        ks = [buf_ref[4, sl, :] for sl in sls]
        vs = [buf_ref[5, sl, :] for sl in sls]
        betas = [buf_ref[6, sl, :] for sl in sls]
        gs = [buf_ref[7, sl, :] for sl in sls]
        gcs = [_dot_exact_lhs(tri, g) for g in gs]
        dmats = [_dot_exact_lhs(tri, g[:, :CHUNK] * upper) for g in gs]
        gammas = [jnp.where(incl, jnp.exp(jnp.where(incl, dm, 0.0)), 0.0) for dm in dmats]
        qbs = [q.astype(BF16) for q in qs]
        kbs = [k.astype(BF16) for k in ks]
        kks = [_dot_nt(kb, kb) for kb in kbs]
        ps = [-jnp.where(strict, beta[:, :CHUNK] * kk * gamma, 0.0) for beta, kk, gamma in zip(betas, kks, gammas)]
        ts = [eye + p for p in ps]
        for _ in range(5):
            ps = [_dot3(p, p) for p in ps]
            ts = [t + _dot3(t, p) for t, p in zip(ts, ps)]
        egcs = [jnp.exp(gc) for gc in gcs]
        rhss = [jnp.concatenate([k * (beta * egc), v * beta], axis=1).astype(BF16)
                for k, v, beta, egc in zip(ks, vs, betas, egcs)]
        wus = []
        for t, rhs in zip(ts, rhss):
            th, tl = _split(t)
            wus.append((_dot(th, rhs) + _dot(tl, rhs)).astype(BF16))
        attns = [jnp.where(incl, _dot_nt(qb, kb) * gamma, 0.0).astype(BF16)
                 for qb, kb, gamma in zip(qbs, kbs, gammas)]
        gls = [gc[CHUNK - 1:CHUNK, :] for gc in gcs]
        k_decs = [(k * jnp.exp(gl - gc)).astype(BF16) for k, gl, gc in zip(ks, gls, gcs)]
        kwus = [_dot_tn(kd, wu) for kd, wu in zip(k_decs, wus)]
        awus = [_dot(a, wu) for a, wu in zip(attns, wus)]
        q_effs = [(q * egc - awu[:, :HEAD_DK]).astype(BF16) for q, egc, awu in zip(qs, egcs, awus)]
        for n, c in enumerate(cs):
            st_b = state.astype(BF16)
            emit(c, _dot(q_effs[n], st_b) + awus[n][:, HEAD_DK:])
            state = (state * jnp.exp(gls[n]) - _dot(kwus[n][:, :HEAD_DK].astype(BF16), st_b)
                     + kwus[n][:, HEAD_DK:])
    return state


def _l2n(x):
    return x * lax.rsqrt(jnp.sum(x * x, axis=-1, keepdims=True) + RMS_EPS)


def _gdn_kernel(q_ref, k_ref, v_ref, z_ref, misc_ref, cw_ref, al_ref, dt_ref, ng_ref, o_ref,
                st_ref, buf_ref, *, nchunks, tblk, group):
    h = pl.program_id(1)

    @pl.when(pl.program_id(2) == 0)
    def _():
        st_ref[...] = jnp.zeros_like(st_ref)
        buf_ref[...] = jnp.zeros_like(buf_ref)

    conv = []
    for n, ref in enumerate((q_ref, k_ref, v_ref)):
        buf_ref[n, 8:, :] = ref[...]
        acc = jnp.zeros((tblk, LANES), F32)
        for j in range(B_CONV):
            acc = acc + cw_ref[n, j:j + 1, :] * buf_ref[n, 8 - (B_CONV - 1) + j:8 - (B_CONV - 1) + j + tblk, :]
        buf_ref[n, 0:8, :] = buf_ref[n, tblk:tblk + 8, :]
        conv.append(_silu(acc))
    buf_ref[3, 0:tblk, :] = _l2n(conv[0]) * (HEAD_DK ** -0.5)
    buf_ref[4, 0:tblk, :] = _l2n(conv[1])
    buf_ref[5, 0:tblk, :] = conv[2]

    lane_r = _iota((LANES, LANES), 0)
    sel_beta = jnp.where(lane_r == MISC_BETA + h, 1.0, 0.0).astype(BF16)
    sel_dec = jnp.where(lane_r == MISC_DECAY + h, 1.0, 0.0).astype(BF16)
    mh, ml = _split(misc_ref[...])
    beta_logit = _dot(mh, sel_beta) + _dot(ml, sel_beta)
    dec_in = _dot(mh, sel_dec) + _dot(ml, sel_dec) + dt_ref[...]
    softplus = jnp.maximum(dec_in, 0.0) + jnp.log(1.0 + jnp.exp(-jnp.abs(dec_in)))
    buf_ref[6, 0:tblk, :] = _sigmoid(beta_logit)
    buf_ref[7, 0:tblk, :] = -jnp.exp(al_ref[...]) * softplus

    ng = ng_ref[...]

    def emit(c, o):
        sl = slice(c * CHUNK, (c + 1) * CHUNK)
        o_ref[sl, :] = _rms_gate(o, ng, z_ref[sl, :]).astype(o_ref.dtype)

    st_ref[...] = _gdn_block(buf_ref, st_ref[...], nchunks, group, emit)


def _gdn(proj, conv_w, a_log, dt_bias, norm_g, bsz, seq, tblk, group):
    nt = seq // tblk

    def col(base):
        return pl.BlockSpec((tblk, LANES), lambda b, h, t: (b * nt + t, base + h))

    def per_head(shape):
        return pl.BlockSpec((None,) + shape, lambda b, h, t: (h,) + (0,) * len(shape))

    return pl.pallas_call(
        functools.partial(_gdn_kernel, nchunks=tblk // CHUNK, tblk=tblk, group=group),
        grid=(bsz, B_HEADS, nt),
        in_specs=[col(COL_BQ), col(COL_BK), col(COL_BV), col(COL_BZ),
                  pl.BlockSpec((tblk, LANES), lambda b, h, t: (b * nt + t, COL_MISC)),
                  per_head((3, B_CONV, LANES)), per_head((1, LANES)), per_head((1, LANES)),
                  _full_spec((1, B_DV))],
        out_specs=pl.BlockSpec((tblk, B_DV), lambda b, h, t: (b * nt + t, h)),
        out_shape=jax.ShapeDtypeStruct((bsz * seq, B_HEADS * B_DV), BF16),
        scratch_shapes=[pltpu.VMEM((HEAD_DK, B_DV), F32), pltpu.VMEM((8, tblk + 8, LANES), F32)],
        compiler_params=_cparams(("arbitrary", "arbitrary", "arbitrary")),
        name="gdn",
    )(proj, proj, proj, proj, proj, conv_w, a_log, dt_bias, norm_g.reshape(1, B_DV))


def _xattn_kernel(xb_ref, xf_ref, wq_ref, kv_ref, wo_ref, g_ref, b_ref, of_ref, ob_ref, *, alpha):
    q = _dot(xb_ref[...], wq_ref[...])
    width = X_HEADS * X_HEAD_DIM
    outs = []
    for hd in range(X_HEADS):
        lo = hd * X_HEAD_DIM
        qh = q[:, lo:lo + X_HEAD_DIM].astype(BF16)
        kh = kv_ref[:, lo:lo + X_HEAD_DIM]
        vh = kv_ref[:, width + lo:width + lo + X_HEAD_DIM]
        s = _dot_nt(qh, kh) * (X_HEAD_DIM ** -0.5)
        s = s - jnp.max(s, axis=-1, keepdims=True)
        p = jnp.exp(s)
        p = p / jnp.sum(p, axis=-1, keepdims=True)
        outs.append(_dot(p.astype(BF16), vh))
    o = jnp.concatenate(outs, axis=1).astype(BF16)
    tq, d = of_ref.shape
    for c in range(0, d, COL_SLICE):
        of_ref[:, c:c + COL_SLICE] = _dot(o, wo_ref[:, c:c + COL_SLICE])
    for r in range(0, tq, ROW_SLICE):
        sl = slice(r, r + ROW_SLICE)
        y = _ln_rows(alpha * xf_ref[sl, :] + of_ref[sl, :], g_ref[...], b_ref[...])
        of_ref[sl, :] = y
        ob_ref[sl, :] = y.astype(BF16)


def _xattn(xb, xf, wq, kv, wo, layer, g, b, alpha, bsz, seq, tq):
    d = xf.shape[1]
    nt = seq // tq
    width = X_HEADS * X_HEAD_DIM
    n_mem = kv.shape[1]
    return pl.pallas_call(
        functools.partial(_xattn_kernel, alpha=alpha),
        grid=(bsz, nt),
        in_specs=[pl.BlockSpec((tq, d), lambda bb, t: (bb * nt + t, 0)),
                  pl.BlockSpec((tq, d), lambda bb, t: (bb * nt + t, 0)),
                  pl.BlockSpec((None, d, width), lambda bb, t: (layer, 0, 0), pipeline_mode=pl.Buffered(1)),
                  pl.BlockSpec((None, n_mem, 2 * width), lambda bb, t: (bb, 0, 0)),
                  pl.BlockSpec((None, width, d), lambda bb, t: (layer, 0, 0), pipeline_mode=pl.Buffered(1)),
                  pl.BlockSpec((1, d), lambda bb, t: (0, 0)),
                  pl.BlockSpec((1, d), lambda bb, t: (0, 0))],
        out_specs=[pl.BlockSpec((tq, d), lambda bb, t: (bb * nt + t, 0)),
                   pl.BlockSpec((tq, d), lambda bb, t: (bb * nt + t, 0))],
        out_shape=[jax.ShapeDtypeStruct((bsz * seq, d), F32), jax.ShapeDtypeStruct((bsz * seq, d), BF16)],
        compiler_params=_cparams(("arbitrary", "arbitrary")),
        name="xattn",
    )(xb, xf, wq, kv, wo, g.reshape(1, d), b.reshape(1, d))


def _router_kernel(xb_ref, w_ref, b_ref, o_ref):
    logits = _dot(xb_ref[...], w_ref[...]) + b_ref[...]
    lane = _iota(logits.shape, 1).astype(F32)
    neg = jnp.float32(-3.0e38)
    picked = []
    vals = []
    work = logits
    for _ in range(TOP_K):
        mx = jnp.max(work, axis=-1, keepdims=True)
        idx = jnp.min(jnp.where(work == mx, lane, float(LANES)), axis=-1, keepdims=True)
        sel = lane == idx
        picked.append(sel)
        vals.append(mx)
        work = jnp.where(sel, neg, work)
    es = [jnp.exp(vv - vals[0]) for vv in vals]
    inv = 1.0 / (es[0] + es[1] + es[2] + es[3])
    comb = jnp.zeros_like(logits)
    for sel, e in zip(picked, es):
        comb = comb + jnp.where(sel, e * inv, 0.0)
    o_ref[...] = comb


def _router(xb, w_pad, b_pad, tm):
    m, d = xb.shape
    return pl.pallas_call(
        _router_kernel,
        grid=(m // tm,),
        in_specs=[pl.BlockSpec((tm, d), lambda i: (i, 0)),
                  pl.BlockSpec((d, LANES), lambda i: (0, 0)),
                  pl.BlockSpec((1, LANES), lambda i: (0, 0))],
        out_specs=pl.BlockSpec((tm, LANES), lambda i: (i, 0)),
        out_shape=jax.ShapeDtypeStruct((m, LANES), F32),
        compiler_params=_cparams(("arbitrary",)),
        name="router",
    )(xb, w_pad, b_pad)


def _moe_kernel(xb_ref, xf_ref, comb_ref, wg_ref, bg_ref, wu_ref, bu_ref, wd_ref, bd_ref, g_ref, b_ref,
                of_ref, ob_ref, *, alpha, ne):
    e = pl.program_id(1)
    tm, d = of_ref.shape
    rows = [slice(r, r + ROW_SLICE) for r in range(0, tm, ROW_SLICE)]
    cols = [slice(c, c + COL_SLICE) for c in range(0, d, COL_SLICE)]

    @pl.when(e == 0)
    def _():
        of_ref[...] = jnp.zeros_like(of_ref)

    sel = jnp.where(_iota((LANES, EXPERT_FF), 0) == e, 1.0, 0.0).astype(BF16)
    xb = xb_ref[...]
    gate = jnp.minimum(_dot(xb, wg_ref[...]) + bg_ref[...], SWIGLU_LIMIT)
    lin = jnp.clip(_dot(xb, wu_ref[...]) + bu_ref[...], -SWIGLU_LIMIT, SWIGLU_LIMIT)
    hid = (lin + 1.0) * gate * _sigmoid(SWIGLU_ALPHA * gate)
    ch, cl = _split(comb_ref[...])
    weight = _dot(ch, sel) + _dot(cl, sel)
    hw = (hid * weight).astype(BF16)
    for sl in cols:
        of_ref[:, sl] += _dot(hw, wd_ref[:, sl])

    @pl.when(e == ne - 1)
    def _():
        for sl in rows:
            y = of_ref[sl, :] + _dot(comb_ref[sl, :].astype(BF16), bd_ref[...])
            y = _ln_rows(alpha * xf_ref[sl, :] + y, g_ref[...], b_ref[...])
            of_ref[sl, :] = y
            ob_ref[sl, :] = y.astype(BF16)


def _moe(xb, xf, comb, wg, bg, wu, bu, wd, layer, ne, bd_pad, g, b, alpha, tm):
    m, d = xf.shape
    ff = wg.shape[2]
    base = layer * ne
    once = pl.Buffered(1)
    return pl.pallas_call(
        functools.partial(_moe_kernel, alpha=alpha, ne=ne),
        grid=(m // tm, ne),
        in_specs=[pl.BlockSpec((tm, d), lambda i, e: (i, 0), pipeline_mode=once),
                  pl.BlockSpec((tm, d), lambda i, e: (i, 0), pipeline_mode=once),
                  pl.BlockSpec((tm, LANES), lambda i, e: (i, 0), pipeline_mode=once),
                  pl.BlockSpec((None, d, ff), lambda i, e: (base + e, 0, 0)),
                  pl.BlockSpec((None, 1, ff), lambda i, e: (base + e, 0, 0)),
                  pl.BlockSpec((None, d, ff), lambda i, e: (base + e, 0, 0)),
                  pl.BlockSpec((None, 1, ff), lambda i, e: (base + e, 0, 0)),
                  pl.BlockSpec((None, ff, d), lambda i, e: (base + e, 0, 0)),
                  pl.BlockSpec((LANES, d), lambda i, e: (0, 0), pipeline_mode=once),
                  pl.BlockSpec((1, d), lambda i, e: (0, 0)),
                  pl.BlockSpec((1, d), lambda i, e: (0, 0))],
        out_specs=[pl.BlockSpec((tm, d), lambda i, e: (i, 0)),
                   pl.BlockSpec((tm, d), lambda i, e: (i, 0))],
        out_shape=[jax.ShapeDtypeStruct((m, d), F32), jax.ShapeDtypeStruct((m, d), BF16)],
        compiler_params=_cparams(("arbitrary", "arbitrary")),
        name="moe",
    )(xb, xf, comb, wg, bg, wu, bu, wd, bd_pad, g.reshape(1, d), b.reshape(1, d))


def _pick(n, prefs):
    for p in prefs:
        if n % p == 0:
            return p
    raise ValueError(f"no tile in {prefs} divides {n}")


def kernel(x, mem, w_in, hgrn_lb_raw, hgrn_norm_g, gdn_conv_w, gdn_a_log, gdn_dt_bias, gdn_norm_g, gla_w_up, gla_b_up, gla_norm_g, w_out, ln_mix_g, ln_mix_b, mem_ln_g, mem_ln_b, xattn_wq, xattn_wkv, xattn_wo, ln_xattn_g, ln_xattn_b, w_router, b_router, w_gate, b_gate, w_up, b_up, w_down, b_down, ln_ffn_g, ln_ffn_b):
    bsz, seq, d = x.shape
    depth = w_in.shape[0]
    n_mem = mem.shape[1]
    ntok = bsz * seq
    alpha = float((2 * depth) ** 0.25)
    assert seq % CHUNK == 0 and d % LANES == 0

    tblk = _pick(seq, (512, 256, 128, 64))
    group = min(4, tblk // CHUNK)
    tm_big = _pick(ntok, (512, 256, 128, 64))
    tm_mid = _pick(ntok, (256, 128, 64))

    a_w, b_w, c_w = A_HEADS * HEAD_DK, B_HEADS * HEAD_DK, C_HEADS * HEAD_DK
    o_bb = 4 * a_w + (2 * b_w + B_HEADS * B_DV) + B_HEADS * B_DV
    o_cq = o_bb + 2 * B_HEADS
    o_clr = o_cq + 2 * c_w + 2 * C_HEADS * C_DV
    n_cols = o_clr + C_RANK
    assert w_in.shape[2] == n_cols
    main_cols = o_bb + (o_clr - o_cq)
    assert main_cols == COL_MISC * LANES
    w_in_p = jnp.concatenate(
        [w_in[:, :, :o_bb], w_in[:, :, o_cq:o_clr], w_in[:, :, o_bb:o_cq], w_in[:, :, o_clr:],
         jnp.zeros((depth, d, IN_COLS_PAD - n_cols), w_in.dtype)], axis=2).astype(BF16)
    w_out_b = w_out.astype(BF16)
    wq_b, wkv_b, wo_b = xattn_wq.astype(BF16), xattn_wkv.astype(BF16), xattn_wo.astype(BF16)
    wr_p = jnp.concatenate([w_router, jnp.zeros((depth, d, LANES - N_EXPERTS), F32)], axis=2).astype(BF16)
    br_p = jnp.concatenate([b_router, jnp.full((depth, LANES - N_EXPERTS), -1.0e30, F32)], axis=1)
    ne, ff = w_gate.shape[1], w_gate.shape[3]
    wg_b = w_gate.astype(BF16).reshape(depth * ne, d, ff)
    wu_b = w_up.astype(BF16).reshape(depth * ne, d, ff)
    wd_b = w_down.astype(BF16).reshape(depth * ne, ff, d)
    bg_r = b_gate.reshape(depth * ne, 1, ff)
    bu_r = b_up.reshape(depth * ne, 1, ff)
    bd_p = jnp.concatenate([b_down, jnp.zeros((depth, LANES - N_EXPERTS, d), F32)], axis=1).astype(BF16)
    gla_wup_p = jnp.zeros((depth, LANES, c_w), F32).at[:, MISC_CODE:MISC_CODE + C_RANK, :].set(gla_w_up)
    gla_wup_p = gla_wup_p.reshape(depth, LANES, C_HEADS, HEAD_DK).transpose(0, 2, 1, 3).astype(BF16)
    gla_bup_r = gla_b_up.reshape(depth, C_HEADS, 1, HEAD_DK)
    conv_r = gdn_conv_w.reshape(depth, B_CONV, 3, B_HEADS, HEAD_DK).transpose(0, 3, 2, 1, 4)
    a_log_r = jnp.broadcast_to(gdn_a_log[:, :, None, None], (depth, B_HEADS, 1, LANES))
    dt_r = jnp.broadcast_to(gdn_dt_bias[:, :, None, None], (depth, B_HEADS, 1, LANES))
    lb_all = jnp.cumsum(jax.nn.softmax(hgrn_lb_raw.astype(F32), axis=0), axis=0)
    lb_r = (lb_all - lb_all[0:1]).reshape(depth, A_HEADS, 1, HEAD_DK)
    rsel = _range_selectors()
    masks = _level_masks()

    tm_mem = _pick(bsz * n_mem, (256, 128, 64, 8))
    mem_n = _layer_norm(mem.reshape(bsz * n_mem, d), mem_ln_g, mem_ln_b, BF16, tm_mem)

    xf = x.reshape(ntok, d)
    xb = xf.astype(BF16)
    for l in range(depth):
        proj = _matmul(xb, w_in_p, l, F32, tm_big, 1536)
        oa = _hgrn(proj, lb_r[l], hgrn_norm_g[l], rsel, masks, bsz, seq, tblk, group)
        ob = _gdn(proj, conv_r[l], a_log_r[l], dt_r[l], gdn_norm_g[l], bsz, seq, tblk, group)
        oc = _gla(proj, gla_wup_p[l], gla_bup_r[l], gla_norm_g[l], rsel, masks, bsz, seq, tblk, group)
        xf, xb = _proj_ln(oa, ob, oc, w_out_b, l, xf, ln_mix_g[l], ln_mix_b[l], alpha, tm_big, 512)

        kv = _matmul(mem_n, wkv_b, l, BF16, tm_mem, 2 * X_HEADS * X_HEAD_DIM)
        xf, xb = _xattn(xb, xf, wq_b, kv.reshape(bsz, n_mem, -1), wo_b, l, ln_xattn_g[l], ln_xattn_b[l],
                        alpha, bsz, seq, _pick(seq, (256, 128, 64)))

        comb = _router(xb, wr_p[l], br_p[l, None, :], tm_big)
        xf, xb = _moe(xb, xf, comb, wg_b, bg_r, wu_b, bu_r, wd_b, l, ne, bd_p[l], ln_ffn_g[l], ln_ffn_b[l],
                      alpha, tm_big)
    return xf.reshape(bsz, seq, d)
```

```python
import functools

import numpy as np
import jax
import jax.numpy as jnp
from jax import lax
from jax.experimental import pallas as pl
from jax.experimental.pallas import tpu as pltpu

F32 = jnp.float32
BF16 = jnp.bfloat16

LANES = 128
CHUNK = 64
VMEM_LIMIT = 60 * 1024 * 1024
ROW_SLICE = 256
COL_SLICE = 1024

A_HEADS, B_HEADS, C_HEADS = 8, 12, 6
HEAD_DK = 128
A_DV, B_DV, C_DV = 128, 128, 256
B_CONV = 4
C_RANK = 16
C_TAU = 16.0
X_HEADS, X_HEAD_DIM = 4, 128
N_EXPERTS, TOP_K, EXPERT_FF = 32, 4, 256
SWIGLU_LIMIT, SWIGLU_ALPHA = 7.0, 1.702
LN_EPS, RMS_EPS = 1e-5, 1e-6

COL_AQ, COL_AF, COL_AI, COL_AG = 0, 8, 16, 24
COL_BQ, COL_BK, COL_BV, COL_BZ = 32, 44, 56, 68
COL_CQ, COL_CK, COL_CV, COL_CR = 80, 86, 92, 104
COL_MISC = 116
MISC_BETA, MISC_DECAY, MISC_CODE = 0, 12, 24
IN_COLS_PAD = 120 * LANES


def _cparams(sem):
    return pltpu.CompilerParams(dimension_semantics=sem, vmem_limit_bytes=VMEM_LIMIT)


def _dot(a, b):
    return jnp.dot(a, b, preferred_element_type=F32)


def _dot_nt(a, b):
    return lax.dot_general(a, b, (((1,), (1,)), ((), ())), preferred_element_type=F32)


def _dot_tn(a, b):
    return lax.dot_general(a, b, (((0,), (0,)), ((), ())), preferred_element_type=F32)


def _split(x):
    hi = x.astype(BF16)
    lo = (x - hi.astype(F32)).astype(BF16)
    return hi, lo


def _dot_exact_lhs(sel2, x):
    hi, lo = _split(x)
    return _dot(sel2, jnp.concatenate([hi, lo], axis=0))


def _sigmoid(x):
    return 1.0 / (1.0 + jnp.exp(-x))


def _silu(x):
    return x * _sigmoid(x)


def _iota(shape, dim):
    return lax.broadcasted_iota(jnp.int32, shape, dim)


def _full_spec(shape):
    return pl.BlockSpec(shape, lambda *_: (0,) * len(shape))


def _mm_kernel(a_ref, w_ref, o_ref):
    o_ref[...] = _dot(a_ref[...], w_ref[...]).astype(o_ref.dtype)


def _matmul(a, w, layer, out_dtype, tm, tn):
    m, k = a.shape
    n = w.shape[2]
    return pl.pallas_call(
        _mm_kernel,
        grid=(n // tn, m // tm),
        in_specs=[pl.BlockSpec((tm, k), lambda j, i: (i, 0)),
                  pl.BlockSpec((None, k, tn), lambda j, i: (layer, 0, j))],
        out_specs=pl.BlockSpec((tm, tn), lambda j, i: (i, j)),
        out_shape=jax.ShapeDtypeStruct((m, n), out_dtype),
        compiler_params=_cparams(("arbitrary", "arbitrary")),
        name="matmul",
    )(a, w)


def _ln_rows(y, g, b):
    mu = jnp.mean(y, axis=-1, keepdims=True)
    d = y - mu
    var = jnp.mean(d * d, axis=-1, keepdims=True)
    return d * lax.rsqrt(var + LN_EPS) * g + b


def _ln_kernel(x_ref, g_ref, b_ref, o_ref):
    o_ref[...] = _ln_rows(x_ref[...], g_ref[...], b_ref[...]).astype(o_ref.dtype)


def _layer_norm(x, g, b, out_dtype, tm):
    m, d = x.shape
    return pl.pallas_call(
        _ln_kernel,
        grid=(m // tm,),
        in_specs=[pl.BlockSpec((tm, d), lambda i: (i, 0)),
                  pl.BlockSpec((1, d), lambda i: (0, 0)),
                  pl.BlockSpec((1, d), lambda i: (0, 0))],
        out_specs=pl.BlockSpec((tm, d), lambda i: (i, 0)),
        out_shape=jax.ShapeDtypeStruct((m, d), out_dtype),
        compiler_params=_cparams(("arbitrary",)),
        name="layer_norm",
    )(x, g.reshape(1, d), b.reshape(1, d))


def _proj_ln_kernel(a0_ref, a1_ref, a2_ref, w_ref, x_ref, g_ref, b_ref, of_ref, ob_ref, *, alpha, bounds):
    kk = pl.program_id(1)
    n0, n1, nk = bounds
    tm, d = of_ref.shape
    rows = [slice(r, r + ROW_SLICE) for r in range(0, tm, ROW_SLICE)]
    cols = [slice(c, c + COL_SLICE) for c in range(0, d, COL_SLICE)]

    @pl.when(kk == 0)
    def _():
        of_ref[...] = jnp.zeros_like(of_ref)

    def accumulate(a_ref):
        a = a_ref[...]
        for sl in cols:
            of_ref[:, sl] += _dot(a, w_ref[:, sl])

    pl.when(kk < n0)(lambda: accumulate(a0_ref))
    pl.when((kk >= n0) & (kk < n1))(lambda: accumulate(a1_ref))
    pl.when(kk >= n1)(lambda: accumulate(a2_ref))

    @pl.when(kk == nk - 1)
    def _():
        for sl in rows:
            y = _ln_rows(alpha * x_ref[sl, :] + of_ref[sl, :], g_ref[...], b_ref[...])
            of_ref[sl, :] = y
            ob_ref[sl, :] = y.astype(BF16)


def _proj_ln(a0, a1, a2, w, layer, x, g, b, alpha, tm, tk):
    m = a0.shape[0]
    d = w.shape[2]
    n0 = a0.shape[1] // tk
    n1 = n0 + a1.shape[1] // tk
    nk = n1 + a2.shape[1] // tk
    assert nk * tk == w.shape[1]
    return pl.pallas_call(
        functools.partial(_proj_ln_kernel, alpha=alpha, bounds=(n0, n1, nk)),
        grid=(m // tm, nk),
        in_specs=[pl.BlockSpec((tm, tk), lambda i, kk: (i, jnp.minimum(kk, n0 - 1))),
                  pl.BlockSpec((tm, tk), lambda i, kk: (i, jnp.clip(kk - n0, 0, n1 - n0 - 1))),
                  pl.BlockSpec((tm, tk), lambda i, kk: (i, jnp.clip(kk - n1, 0, nk - n1 - 1))),
                  pl.BlockSpec((None, tk, d), lambda i, kk: (layer, kk, 0)),
                  pl.BlockSpec((tm, d), lambda i, kk: (i, 0)),
                  pl.BlockSpec((1, d), lambda i, kk: (0, 0)),
                  pl.BlockSpec((1, d), lambda i, kk: (0, 0))],
        out_specs=[pl.BlockSpec((tm, d), lambda i, kk: (i, 0)),
                   pl.BlockSpec((tm, d), lambda i, kk: (i, 0))],
        out_shape=[jax.ShapeDtypeStruct((m, d), F32), jax.ShapeDtypeStruct((m, d), BF16)],
        compiler_params=_cparams(("arbitrary", "arbitrary")),
        name="proj_ln",
    )(a0, a1, a2, w, x, g.reshape(1, d), b.reshape(1, d))


_LEVELS = (32, 16, 8, 4, 2, 1)
_N_SEL = len(_LEVELS) + 2


def _range_selectors():
    i = np.arange(CHUNK)[:, None]
    t = np.arange(CHUNK)[None, :]
    blocks = [t <= i]
    for s in _LEVELS:
        m = (i & ~(2 * s - 1)) | (s - 1)
        later = (i & s) != 0
        blocks.append(np.where(later, (t > m) & (t <= i), (t > i) & (t <= m)))
    blocks.append(t > i)
    sel = np.concatenate(blocks, axis=0).astype(np.float32)
    return jnp.asarray(np.concatenate([sel, sel], axis=1), dtype=BF16)


def _level_masks():
    i = np.arange(CHUNK)[:, None]
    j = np.arange(CHUNK)[None, :]
    masks = [i == j]
    for s in _LEVELS:
        sh = s.bit_length()
        masks.append(((i >> sh) == (j >> sh)) & ((i & s) != 0) & ((j & s) == 0))
    return jnp.asarray(np.stack(masks).astype(np.float32))


def _diag_gated_block(q_ref, k_ref, v_ref, la_ref, e_ref, rsel_ref, mask_ref, state_t, nchunks, group, emit):
    dk = q_ref.shape[1]
    row = _iota((CHUNK, dk), 0)
    laters = [(row & s) != 0 for s in _LEVELS]
    rsel = rsel_ref[...]
    span = _N_SEL * CHUNK
    for g0 in range(0, nchunks, group):
        cs = list(range(g0, min(g0 + group, nchunks)))
        for c in cs:
            e_ref[c * span:(c + 1) * span, :] = _dot_exact_lhs(rsel, la_ref[c * CHUNK:(c + 1) * CHUNK, :])
        qs = [q_ref[c * CHUNK:(c + 1) * CHUNK, :] for c in cs]
        ks = [k_ref[c * CHUNK:(c + 1) * CHUNK, :] for c in cs]
        vbs = [v_ref[c * CHUNK:(c + 1) * CHUNK, :].astype(BF16) for c in cs]
        scores = [_dot_nt(q.astype(BF16), k.astype(BF16)) * mask_ref[0] for q, k in zip(qs, ks)]
        for n in range(len(_LEVELS)):
            zs = [(jnp.where(laters[n], q, k)
                   * jnp.exp(e_ref[c * span + (n + 1) * CHUNK:c * span + (n + 2) * CHUNK, :])).astype(BF16)
                  for c, q, k in zip(cs, qs, ks)]
            scores = [sc + _dot_nt(z, z) * mask_ref[n + 1] for sc, z in zip(scores, zs)]
        o_intra = [_dot(sc.astype(BF16), vb) for sc, vb in zip(scores, vbs)]
        q_dec = [(q * jnp.exp(e_ref[c * span:c * span + CHUNK, :])).astype(BF16) for c, q in zip(cs, qs)]
        k_dec = [(k * jnp.exp(e_ref[c * span + 7 * CHUNK:(c + 1) * span, :])).astype(BF16) for c, k in zip(cs, ks)]
        ds_t = [_dot_tn(vb, kd) for vb, kd in zip(vbs, k_dec)]
        for n, c in enumerate(cs):
            emit(c, o_intra[n] + _dot_nt(q_dec[n], state_t.astype(BF16)))
            decay = jnp.exp(e_ref[c * span + CHUNK - 1:c * span + CHUNK, :])
            state_t = state_t * decay + ds_t[n]
    return state_t


def _rms_gate(o, g, gate):
    y = o * lax.rsqrt(jnp.mean(o * o, axis=-1, keepdims=True) + RMS_EPS)
    return y * g * _silu(gate)


def _hgrn_kernel(q_ref, f_ref, i_ref, g_ref, lb_ref, ng_ref, rsel_ref, mask_ref, o_ref,
                 st_ref, qs_ref, ks_ref, la_ref, e_ref, *, nchunks, group):
    @pl.when(pl.program_id(2) == 0)
    def _():
        st_ref[...] = jnp.zeros_like(st_ref)

    lb = lb_ref[0]
    f = lb + (1.0 - lb) * _sigmoid(f_ref[...])
    qs_ref[...] = _silu(q_ref[...])
    ks_ref[...] = 1.0 - f
    la_ref[...] = jnp.log(f)
    ng = ng_ref[...]

    def emit(c, o):
        sl = slice(c * CHUNK, (c + 1) * CHUNK)
        o_ref[sl, :] = _rms_gate(o, ng, g_ref[sl, :]).astype(o_ref.dtype)

    st_ref[...] = _diag_gated_block(qs_ref, ks_ref, i_ref, la_ref, e_ref, rsel_ref, mask_ref, st_ref[...],
                                    nchunks, group, emit)


def _hgrn(proj, lb, norm_g, rsel, masks, bsz, seq, tblk, group):
    nt = seq // tblk

    def col(base):
        return pl.BlockSpec((tblk, LANES), lambda b, h, t: (b * nt + t, base + h))

    return pl.pallas_call(
        functools.partial(_hgrn_kernel, nchunks=tblk // CHUNK, group=group),
        grid=(bsz, A_HEADS, nt),
        in_specs=[col(COL_AQ), col(COL_AF), col(COL_AI), col(COL_AG),
                  pl.BlockSpec((1, 1, LANES), lambda b, h, t: (h, 0, 0)),
                  _full_spec((1, A_DV)), _full_spec(rsel.shape), _full_spec(masks.shape)],
        out_specs=pl.BlockSpec((tblk, A_DV), lambda b, h, t: (b * nt + t, h)),
        out_shape=jax.ShapeDtypeStruct((bsz * seq, A_HEADS * A_DV), BF16),
        scratch_shapes=[pltpu.VMEM((A_DV, HEAD_DK), F32), pltpu.VMEM((tblk, HEAD_DK), F32),
                        pltpu.VMEM((tblk, HEAD_DK), F32), pltpu.VMEM((tblk, HEAD_DK), F32),
                        pltpu.VMEM((tblk * _N_SEL, HEAD_DK), F32)],
        compiler_params=_cparams(("arbitrary", "arbitrary", "arbitrary")),
        name="hgrn2",
    )(proj, proj, proj, proj, lb, norm_g.reshape(1, A_DV), rsel, masks)


def _gla_kernel(q_ref, k_ref, v_ref, r_ref, misc_ref, wup_ref, bup_ref, ng_ref, rsel_ref, mask_ref, o_ref,
                st_ref, qs_ref, la_ref, e_ref, *, nchunks, group):
    @pl.when(pl.program_id(2) == 0)
    def _():
        st_ref[...] = jnp.zeros_like(st_ref)

    z = _dot(misc_ref[...].astype(BF16), wup_ref[0]) + bup_ref[0]
    la_ref[...] = (jnp.minimum(z, 0.0) - jnp.log(1.0 + jnp.exp(-jnp.abs(z)))) * (1.0 / C_TAU)
    qs_ref[...] = q_ref[...] * (HEAD_DK ** -0.5)
    ng = ng_ref[...]

    def emit(c, o):
        sl = slice(c * CHUNK, (c + 1) * CHUNK)
        o_ref[sl, :] = _rms_gate(o, ng, r_ref[sl, :]).astype(o_ref.dtype)

    st_ref[...] = _diag_gated_block(qs_ref, k_ref, v_ref, la_ref, e_ref, rsel_ref, mask_ref, st_ref[...],
                                    nchunks, group, emit)


def _gla(proj, w_up_pad, b_up, norm_g, rsel, masks, bsz, seq, tblk, group):
    nt = seq // tblk

    def col(base):
        return pl.BlockSpec((tblk, LANES), lambda b, h, t: (b * nt + t, base + h))

    def col2(base):
        return pl.BlockSpec((tblk, C_DV), lambda b, h, t: (b * nt + t, base // 2 + h))

    return pl.pallas_call(
        functools.partial(_gla_kernel, nchunks=tblk // CHUNK, group=group),
        grid=(bsz, C_HEADS, nt),
        in_specs=[col(COL_CQ), col(COL_CK), col2(COL_CV), col2(COL_CR),
                  pl.BlockSpec((tblk, LANES), lambda b, h, t: (b * nt + t, COL_MISC)),
                  pl.BlockSpec((1, LANES, LANES), lambda b, h, t: (h, 0, 0)),
                  pl.BlockSpec((1, 1, LANES), lambda b, h, t: (h, 0, 0)),
                  _full_spec((1, C_DV)), _full_spec(rsel.shape), _full_spec(masks.shape)],
        out_specs=pl.BlockSpec((tblk, C_DV), lambda b, h, t: (b * nt + t, h)),
        out_shape=jax.ShapeDtypeStruct((bsz * seq, C_HEADS * C_DV), BF16),
        scratch_shapes=[pltpu.VMEM((C_DV, HEAD_DK), F32), pltpu.VMEM((tblk, HEAD_DK), F32),
                        pltpu.VMEM((tblk, HEAD_DK), F32), pltpu.VMEM((tblk * _N_SEL, HEAD_DK), F32)],
        compiler_params=_cparams(("arbitrary", "arbitrary", "arbitrary")),
        name="gla",
    )(proj, proj, proj, proj, proj, w_up_pad, b_up, norm_g.reshape(1, C_DV), rsel, masks)


def _times_p(lefts, p):
    n = p.shape[0]
    ph, pl_ = _split(p)
    halves = [_split(x) for x in lefts]
    by_hi = _dot(jnp.concatenate([h for pair in halves for h in pair], axis=0), ph)
    by_lo = _dot(jnp.concatenate([hi for hi, _ in halves], axis=0), pl_)
    return [by_hi[2 * i * n:(2 * i + 1) * n] + by_hi[(2 * i + 1) * n:(2 * i + 2) * n] + by_lo[i * n:(i + 1) * n]
            for i in range(len(lefts))]


def _unit_lower_inverses(neg_ms):
    shape = neg_ms[0].shape
    eye = jnp.where(_iota(shape, 0) == _iota(shape, 1), 1.0, 0.0)
    ts = [eye + m for m in neg_ms]
    ps = [_times_p([m], m)[0] for m in neg_ms]
    for _ in range(4):
        prods = [_times_p([t, p], p) for t, p in zip(ts, ps)]
        ts = [t + tp for t, (tp, _) in zip(ts, prods)]
        ps = [pp for _, pp in prods]
    return [t + _times_p([t], p)[0] for t, p in zip(ts, ps)]


def _gdn_block(buf_ref, state, nchunks, group, emit):
    ri = _iota((CHUNK, CHUNK), 0)
    ci = _iota((CHUNK, CHUNK), 1)
    incl = ri >= ci
    strict = ri > ci
    tri = jnp.where(_iota((CHUNK, 2 * CHUNK), 0) >= (_iota((CHUNK, 2 * CHUNK), 1) & (CHUNK - 1)),
                    1.0, 0.0).astype(BF16)
    upper = jnp.where(strict, 1.0, 0.0)
    for g0 in range(0, nchunks, group):
        cs = list(range(g0, min(g0 + group, nchunks)))
        sls = [slice(c * CHUNK, (c + 1) * CHUNK) for c in cs]
        qs = [buf_ref[3, sl, :] for sl in sls]
        ks = [buf_ref[4, sl, :] for sl in sls]
        vs = [buf_ref[5, sl, :] for sl in sls]
        betas = [buf_ref[6, sl, :] for sl in sls]
        gs = [buf_ref[7, sl, :] for sl in sls]
        gcs = [_dot_exact_lhs(tri, g) for g in gs]
        dmats = [_dot_exact_lhs(tri, g[:, :CHUNK] * upper) for g in gs]
        gammas = [jnp.where(incl, jnp.exp(jnp.where(incl, dm, 0.0)), 0.0) for dm in dmats]
        qbs = [q.astype(BF16) for q in qs]
        kbs = [k.astype(BF16) for k in ks]
        kks = [_dot_nt(kb, kb) for kb in kbs]
        ts = _unit_lower_inverses([-jnp.where(strict, beta[:, :CHUNK] * kk * gamma, 0.0)
                                   for beta, kk, gamma in zip(betas, kks, gammas)])
        egcs = [jnp.exp(gc) for gc in gcs]
        rhss = [jnp.concatenate([k * (beta * egc), v * beta], axis=1).astype(BF16)
                for k, v, beta, egc in zip(ks, vs, betas, egcs)]
        wus = []
        for t, rhs in zip(ts, rhss):
            both = _dot(jnp.concatenate(_split(t), axis=0), rhs)
            wus.append((both[:CHUNK] + both[CHUNK:]).astype(BF16))
        attns = [jnp.where(incl, _dot_nt(qb, kb) * gamma, 0.0).astype(BF16)
                 for qb, kb, gamma in zip(qbs, kbs, gammas)]
        gls = [gc[CHUNK - 1:CHUNK, :] for gc in gcs]
        k_decs = [(k * jnp.exp(gl - gc)).astype(BF16) for k, gl, gc in zip(ks, gls, gcs)]
        kwus = [_dot_tn(kd, wu) for kd, wu in zip(k_decs, wus)]
        awus = [_dot(a, wu) for a, wu in zip(attns, wus)]
        q_effs = [(q * egc - awu[:, :HEAD_DK]).astype(BF16) for q, egc, awu in zip(qs, egcs, awus)]
        for n, c in enumerate(cs):
            st_b = state.astype(BF16)
            emit(c, _dot(q_effs[n], st_b) + awus[n][:, HEAD_DK:])
            state = (state * jnp.exp(gls[n]) - _dot(kwus[n][:, :HEAD_DK].astype(BF16), st_b)
                     + kwus[n][:, HEAD_DK:])
    return state


def _l2n(x):
    return x * lax.rsqrt(jnp.sum(x * x, axis=-1, keepdims=True) + RMS_EPS)


def _gdn_kernel(q_ref, k_ref, v_ref, z_ref, misc_ref, cw_ref, al_ref, dt_ref, ng_ref, o_ref,
                st_ref, buf_ref, *, nchunks, tblk, group):
    h = pl.program_id(1)

    @pl.when(pl.program_id(2) == 0)
    def _():
        st_ref[...] = jnp.zeros_like(st_ref)
        buf_ref[...] = jnp.zeros_like(buf_ref)

    conv = []
    for n, ref in enumerate((q_ref, k_ref, v_ref)):
        buf_ref[n, 8:, :] = ref[...]
        acc = jnp.zeros((tblk, LANES), F32)
        for j in range(B_CONV):
            acc = acc + cw_ref[n, j:j + 1, :] * buf_ref[n, 8 - (B_CONV - 1) + j:8 - (B_CONV - 1) + j + tblk, :]
        buf_ref[n, 0:8, :] = buf_ref[n, tblk:tblk + 8, :]
        conv.append(_silu(acc))
    buf_ref[3, 0:tblk, :] = _l2n(conv[0]) * (HEAD_DK ** -0.5)
    buf_ref[4, 0:tblk, :] = _l2n(conv[1])
    buf_ref[5, 0:tblk, :] = conv[2]

    lane_r = _iota((LANES, LANES), 0)
    sel_beta = jnp.where(lane_r == MISC_BETA + h, 1.0, 0.0).astype(BF16)
    sel_dec = jnp.where(lane_r == MISC_DECAY + h, 1.0, 0.0).astype(BF16)
    mh, ml = _split(misc_ref[...])
    beta_logit = _dot(mh, sel_beta) + _dot(ml, sel_beta)
    dec_in = _dot(mh, sel_dec) + _dot(ml, sel_dec) + dt_ref[...]
    softplus = jnp.maximum(dec_in, 0.0) + jnp.log(1.0 + jnp.exp(-jnp.abs(dec_in)))
    buf_ref[6, 0:tblk, :] = _sigmoid(beta_logit)
    buf_ref[7, 0:tblk, :] = -jnp.exp(al_ref[...]) * softplus

    ng = ng_ref[...]

    def emit(c, o):
        sl = slice(c * CHUNK, (c + 1) * CHUNK)
        o_ref[sl, :] = _rms_gate(o, ng, z_ref[sl, :]).astype(o_ref.dtype)

    st_ref[...] = _gdn_block(buf_ref, st_ref[...], nchunks, group, emit)


def _gdn(proj, conv_w, a_log, dt_bias, norm_g, bsz, seq, tblk, group):
    nt = seq // tblk

    def col(base):
        return pl.BlockSpec((tblk, LANES), lambda b, h, t: (b * nt + t, base + h))

    def per_head(shape):
        return pl.BlockSpec((None,) + shape, lambda b, h, t: (h,) + (0,) * len(shape))

    return pl.pallas_call(
        functools.partial(_gdn_kernel, nchunks=tblk // CHUNK, tblk=tblk, group=group),
        grid=(bsz, B_HEADS, nt),
        in_specs=[col(COL_BQ), col(COL_BK), col(COL_BV), col(COL_BZ),
                  pl.BlockSpec((tblk, LANES), lambda b, h, t: (b * nt + t, COL_MISC)),
                  per_head((3, B_CONV, LANES)), per_head((1, LANES)), per_head((1, LANES)),
                  _full_spec((1, B_DV))],
        out_specs=pl.BlockSpec((tblk, B_DV), lambda b, h, t: (b * nt + t, h)),
        out_shape=jax.ShapeDtypeStruct((bsz * seq, B_HEADS * B_DV), BF16),
        scratch_shapes=[pltpu.VMEM((HEAD_DK, B_DV), F32), pltpu.VMEM((8, tblk + 8, LANES), F32)],
        compiler_params=_cparams(("arbitrary", "arbitrary", "arbitrary")),
        name="gdn",
    )(proj, proj, proj, proj, proj, conv_w, a_log, dt_bias, norm_g.reshape(1, B_DV))


def _xattn_kernel(xf_ref, wq_ref, kv_ref, wo_ref, g_ref, b_ref, of_ref, ob_ref, *, alpha):
    q = _dot(xf_ref[...].astype(BF16), wq_ref[...])
    width = X_HEADS * X_HEAD_DIM
    outs = []
    for hd in range(X_HEADS):
        lo = hd * X_HEAD_DIM
        qh = q[:, lo:lo + X_HEAD_DIM].astype(BF16)
        kh = kv_ref[:, lo:lo + X_HEAD_DIM]
        vh = kv_ref[:, width + lo:width + lo + X_HEAD_DIM]
        s = _dot_nt(qh, kh) * (X_HEAD_DIM ** -0.5)
        s = s - jnp.max(s, axis=-1, keepdims=True)
        p = jnp.exp(s)
        p = p / jnp.sum(p, axis=-1, keepdims=True)
        outs.append(_dot(p.astype(BF16), vh))
    o = jnp.concatenate(outs, axis=1).astype(BF16)
    tq, d = of_ref.shape
    for c in range(0, d, COL_SLICE):
        of_ref[:, c:c + COL_SLICE] = _dot(o, wo_ref[:, c:c + COL_SLICE])
    for r in range(0, tq, ROW_SLICE):
        sl = slice(r, r + ROW_SLICE)
        y = _ln_rows(alpha * xf_ref[sl, :] + of_ref[sl, :], g_ref[...], b_ref[...])
        of_ref[sl, :] = y
        ob_ref[sl, :] = y.astype(BF16)


def _xattn(xf, wq, kv, wo, layer, g, b, alpha, bsz, seq, tq):
    d = xf.shape[1]
    nt = seq // tq
    width = X_HEADS * X_HEAD_DIM
    n_mem = kv.shape[1]
    return pl.pallas_call(
        functools.partial(_xattn_kernel, alpha=alpha),
        grid=(bsz, nt),
        in_specs=[pl.BlockSpec((tq, d), lambda bb, t: (bb * nt + t, 0)),
                  pl.BlockSpec((None, d, width), lambda bb, t: (layer, 0, 0), pipeline_mode=pl.Buffered(1)),
                  pl.BlockSpec((None, n_mem, 2 * width), lambda bb, t: (bb, 0, 0)),
                  pl.BlockSpec((None, width, d), lambda bb, t: (layer, 0, 0), pipeline_mode=pl.Buffered(1)),
                  pl.BlockSpec((1, d), lambda bb, t: (0, 0)),
                  pl.BlockSpec((1, d), lambda bb, t: (0, 0))],
        out_specs=[pl.BlockSpec((tq, d), lambda bb, t: (bb * nt + t, 0)),
                   pl.BlockSpec((tq, d), lambda bb, t: (bb * nt + t, 0))],
        out_shape=[jax.ShapeDtypeStruct((bsz * seq, d), F32), jax.ShapeDtypeStruct((bsz * seq, d), BF16)],
        compiler_params=_cparams(("arbitrary", "arbitrary")),
        name="xattn",
    )(xf, wq, kv, wo, g.reshape(1, d), b.reshape(1, d))


def _router_kernel(xb_ref, w_ref, b_ref, o_ref):
    logits = _dot(xb_ref[...], w_ref[...]) + b_ref[...]
    lane = _iota(logits.shape, 1).astype(F32)
    neg = jnp.float32(-3.0e38)
    picked = []
    vals = []
    work = logits
    for _ in range(TOP_K):
        mx = jnp.max(work, axis=-1, keepdims=True)
        idx = jnp.min(jnp.where(work == mx, lane, float(LANES)), axis=-1, keepdims=True)
        sel = lane == idx
        picked.append(sel)
        vals.append(mx)
        work = jnp.where(sel, neg, work)
    es = [jnp.exp(vv - vals[0]) for vv in vals]
    inv = 1.0 / (es[0] + es[1] + es[2] + es[3])
    comb = jnp.zeros_like(logits)
    for sel, e in zip(picked, es):
        comb = comb + jnp.where(sel, e * inv, 0.0)
    o_ref[...] = comb


def _router(xb, w_pad, b_pad, tm):
    m, d = xb.shape
    return pl.pallas_call(
        _router_kernel,
        grid=(m // tm,),
        in_specs=[pl.BlockSpec((tm, d), lambda i: (i, 0)),
                  pl.BlockSpec((d, LANES), lambda i: (0, 0)),
                  pl.BlockSpec((1, LANES), lambda i: (0, 0))],
        out_specs=pl.BlockSpec((tm, LANES), lambda i: (i, 0)),
        out_shape=jax.ShapeDtypeStruct((m, LANES), F32),
        compiler_params=_cparams(("arbitrary",)),
        name="router",
    )(xb, w_pad, b_pad)


def _moe_up_kernel(xb_ref, comb_ref, wg_ref, bg_ref, wu_ref, bu_ref, h_ref):
    e = pl.program_id(1)
    sel = jnp.where(_iota((LANES, EXPERT_FF), 0) == e, 1.0, 0.0).astype(BF16)
    xb = xb_ref[...]
    gate = jnp.minimum(_dot(xb, wg_ref[...]) + bg_ref[...], SWIGLU_LIMIT)
    lin = jnp.clip(_dot(xb, wu_ref[...]) + bu_ref[...], -SWIGLU_LIMIT, SWIGLU_LIMIT)
    hid = (lin + 1.0) * gate * _sigmoid(SWIGLU_ALPHA * gate)
    ch, cl = _split(comb_ref[...])
    weight = _dot(ch, sel) + _dot(cl, sel)
    h_ref[...] = (hid * weight).astype(BF16)


def _moe_up(xb, comb, wg, bg, wu, bu, layer, ne, tm):
    m, d = xb.shape
    ff = wg.shape[2]
    base = layer * ne
    return pl.pallas_call(
        _moe_up_kernel,
        grid=(m // tm, ne),
        in_specs=[pl.BlockSpec((tm, d), lambda i, e: (i, 0)),
                  pl.BlockSpec((tm, LANES), lambda i, e: (i, 0)),
                  pl.BlockSpec((None, d, ff), lambda i, e: (base + e, 0, 0)),
                  pl.BlockSpec((None, 1, ff), lambda i, e: (base + e, 0, 0)),
                  pl.BlockSpec((None, d, ff), lambda i, e: (base + e, 0, 0)),
                  pl.BlockSpec((None, 1, ff), lambda i, e: (base + e, 0, 0))],
        out_specs=pl.BlockSpec((tm, ff), lambda i, e: (i, e)),
        out_shape=jax.ShapeDtypeStruct((m, ne * ff), BF16),
        compiler_params=_cparams(("arbitrary", "arbitrary")),
        name="moe_up",
    )(xb, comb, wg, bg, wu, bu)


def _moe_down_kernel(h_ref, wd_ref, comb_ref, bd_ref, xf_ref, g_ref, b_ref, of_ref, ob_ref, *, alpha, nk):
    kk = pl.program_id(1)
    tm, d = of_ref.shape

    @pl.when(kk == 0)
    def _():
        of_ref[...] = jnp.zeros_like(of_ref)

    h = h_ref[...]
    for c in range(0, d, COL_SLICE):
        of_ref[:, c:c + COL_SLICE] += _dot(h, wd_ref[:, c:c + COL_SLICE])

    @pl.when(kk == nk - 1)
    def _():
        for r in range(0, tm, ROW_SLICE):
            sl = slice(r, r + ROW_SLICE)
            y = of_ref[sl, :] + _dot(comb_ref[sl, :].astype(BF16), bd_ref[...])
            y = _ln_rows(alpha * xf_ref[sl, :] + y, g_ref[...], b_ref[...])
            of_ref[sl, :] = y
            ob_ref[sl, :] = y.astype(BF16)


def _moe_down(h, wd, layer, comb, bd_pad, xf, g, b, alpha, tm, tk):
    m, d = xf.shape
    nk = h.shape[1] // tk
    once = pl.Buffered(1)
    return pl.pallas_call(
        functools.partial(_moe_down_kernel, alpha=alpha, nk=nk),
        grid=(m // tm, nk),
        in_specs=[pl.BlockSpec((tm, tk), lambda i, kk: (i, kk)),
                  pl.BlockSpec((None, tk, d), lambda i, kk: (layer, kk, 0)),
                  pl.BlockSpec((tm, LANES), lambda i, kk: (i, 0), pipeline_mode=once),
                  pl.BlockSpec((LANES, d), lambda i, kk: (0, 0), pipeline_mode=once),
                  pl.BlockSpec((tm, d), lambda i, kk: (i, 0), pipeline_mode=once),
                  pl.BlockSpec((1, d), lambda i, kk: (0, 0)),
                  pl.BlockSpec((1, d), lambda i, kk: (0, 0))],
        out_specs=[pl.BlockSpec((tm, d), lambda i, kk: (i, 0)),
                   pl.BlockSpec((tm, d), lambda i, kk: (i, 0))],
        out_shape=[jax.ShapeDtypeStruct((m, d), F32), jax.ShapeDtypeStruct((m, d), BF16)],
        compiler_params=_cparams(("arbitrary", "arbitrary")),
        name="moe_down",
    )(h, wd, comb, bd_pad, xf, g.reshape(1, d), b.reshape(1, d))


def _pick(n, prefs):
    for p in prefs:
        if n % p == 0:
            return p
    raise ValueError(f"no tile in {prefs} divides {n}")


def kernel(x, mem, w_in, hgrn_lb_raw, hgrn_norm_g, gdn_conv_w, gdn_a_log, gdn_dt_bias, gdn_norm_g, gla_w_up, gla_b_up, gla_norm_g, w_out, ln_mix_g, ln_mix_b, mem_ln_g, mem_ln_b, xattn_wq, xattn_wkv, xattn_wo, ln_xattn_g, ln_xattn_b, w_router, b_router, w_gate, b_gate, w_up, b_up, w_down, b_down, ln_ffn_g, ln_ffn_b):
    bsz, seq, d = x.shape
    depth = w_in.shape[0]
    n_mem = mem.shape[1]
    ntok = bsz * seq
    alpha = float((2 * depth) ** 0.25)
    assert seq % CHUNK == 0 and d % LANES == 0

    tblk = _pick(seq, (512, 256, 128, 64))
    group = min(8, tblk // CHUNK)
    tm_big = _pick(ntok, (512, 256, 128, 64))
    tm_mid = _pick(ntok, (256, 128, 64))
    tm_up = _pick(ntok, (1024, 512, 256, 128, 64))

    a_w, b_w, c_w = A_HEADS * HEAD_DK, B_HEADS * HEAD_DK, C_HEADS * HEAD_DK
    o_bb = 4 * a_w + (2 * b_w + B_HEADS * B_DV) + B_HEADS * B_DV
    o_cq = o_bb + 2 * B_HEADS
    o_clr = o_cq + 2 * c_w + 2 * C_HEADS * C_DV
    n_cols = o_clr + C_RANK
    assert w_in.shape[2] == n_cols
    main_cols = o_bb + (o_clr - o_cq)
    assert main_cols == COL_MISC * LANES
    w_in_p = jnp.concatenate(
        [w_in[:, :, :o_bb], w_in[:, :, o_cq:o_clr], w_in[:, :, o_bb:o_cq], w_in[:, :, o_clr:],
         jnp.zeros((depth, d, IN_COLS_PAD - n_cols), w_in.dtype)], axis=2).astype(BF16)
    w_out_b = w_out.astype(BF16)
    wq_b, wkv_b, wo_b = xattn_wq.astype(BF16), xattn_wkv.astype(BF16), xattn_wo.astype(BF16)
    wr_p = jnp.concatenate([w_router, jnp.zeros((depth, d, LANES - N_EXPERTS), F32)], axis=2).astype(BF16)
    br_p = jnp.concatenate([b_router, jnp.full((depth, LANES - N_EXPERTS), -1.0e30, F32)], axis=1)
    ne, ff = w_gate.shape[1], w_gate.shape[3]
    wg_b = w_gate.astype(BF16).reshape(depth * ne, d, ff)
    wu_b = w_up.astype(BF16).reshape(depth * ne, d, ff)
    wd_b = w_down.astype(BF16).reshape(depth, ne * ff, d)
    bg_r = b_gate.reshape(depth * ne, 1, ff)
    bu_r = b_up.reshape(depth * ne, 1, ff)
    bd_p = jnp.concatenate([b_down, jnp.zeros((depth, LANES - N_EXPERTS, d), F32)], axis=1).astype(BF16)
    gla_wup_p = jnp.zeros((depth, LANES, c_w), F32).at[:, MISC_CODE:MISC_CODE + C_RANK, :].set(gla_w_up)
    gla_wup_p = gla_wup_p.reshape(depth, LANES, C_HEADS, HEAD_DK).transpose(0, 2, 1, 3).astype(BF16)
    gla_bup_r = gla_b_up.reshape(depth, C_HEADS, 1, HEAD_DK)
    conv_r = gdn_conv_w.reshape(depth, B_CONV, 3, B_HEADS, HEAD_DK).transpose(0, 3, 2, 1, 4)
    a_log_r = jnp.broadcast_to(gdn_a_log[:, :, None, None], (depth, B_HEADS, 1, LANES))
    dt_r = jnp.broadcast_to(gdn_dt_bias[:, :, None, None], (depth, B_HEADS, 1, LANES))
    lb_all = jnp.cumsum(jax.nn.softmax(hgrn_lb_raw.astype(F32), axis=0), axis=0)
    lb_r = (lb_all - lb_all[0:1]).reshape(depth, A_HEADS, 1, HEAD_DK)
    rsel = _range_selectors()
    masks = _level_masks()

    tm_mem = _pick(bsz * n_mem, (256, 128, 64, 8))
    mem_n = _layer_norm(mem.reshape(bsz * n_mem, d), mem_ln_g, mem_ln_b, BF16, tm_mem)

    xf = x.reshape(ntok, d)
    xb = xf.astype(BF16)
    for l in range(depth):
        proj = _matmul(xb, w_in_p, l, F32, tm_big, 1536)
        oa = _hgrn(proj, lb_r[l], hgrn_norm_g[l], rsel, masks, bsz, seq, tblk, group)
        ob = _gdn(proj, conv_r[l], a_log_r[l], dt_r[l], gdn_norm_g[l], bsz, seq, tblk, group)
        oc = _gla(proj, gla_wup_p[l], gla_bup_r[l], gla_norm_g[l], rsel, masks, bsz, seq, tblk, group)
        xf, xb = _proj_ln(oa, ob, oc, w_out_b, l, xf, ln_mix_g[l], ln_mix_b[l], alpha, tm_big, 512)

        kv = _matmul(mem_n, wkv_b, l, BF16, tm_mem, 2 * X_HEADS * X_HEAD_DIM)
        xf, xb = _xattn(xf, wq_b, kv.reshape(bsz, n_mem, -1), wo_b, l, ln_xattn_g[l], ln_xattn_b[l],
                        alpha, bsz, seq, _pick(seq, (512, 256, 128, 64)))

        comb = _router(xb, wr_p[l], br_p[l, None, :], tm_big)
        hid = _moe_up(xb, comb, wg_b, bg_r, wu_b, bu_r, l, ne, tm_up)
        xf, xb = _moe_down(hid, wd_b, l, comb, bd_p[l], xf, ln_ffn_g[l], ln_ffn_b[l], alpha, tm_big, 512)
    return xf.reshape(bsz, seq, d)
```

```python
import functools

import numpy as np
import jax
import jax.numpy as jnp
from jax import lax
from jax.experimental import pallas as pl
from jax.experimental.pallas import tpu as pltpu

F32 = jnp.float32
BF16 = jnp.bfloat16

LANES = 128
CHUNK = 64
VMEM_LIMIT = 60 * 1024 * 1024
ROW_SLICE = 256
COL_SLICE = 1024

A_HEADS, B_HEADS, C_HEADS = 8, 12, 6
HEAD_DK = 128
A_DV, B_DV, C_DV = 128, 128, 256
B_CONV = 4
C_RANK = 16
C_TAU = 16.0
X_HEADS, X_HEAD_DIM = 4, 128
N_EXPERTS, TOP_K, EXPERT_FF = 32, 4, 256
SWIGLU_LIMIT, SWIGLU_ALPHA = 7.0, 1.702
LN_EPS, RMS_EPS = 1e-5, 1e-6

COL_AQ, COL_AF, COL_AI, COL_AG = 0, 8, 16, 24
COL_BQ, COL_BK, COL_BV, COL_BZ = 32, 44, 56, 68
MAIN_COLS = 80 * LANES
COL_CQ, COL_CK, COL_CV, COL_CR = 0, 6, 12, 24
MISC_BETA, MISC_DECAY, MISC_CODE = 0, 12, 24


def _cparams(sem):
    return pltpu.CompilerParams(dimension_semantics=sem, vmem_limit_bytes=VMEM_LIMIT)


def _dot(a, b):
    return jnp.dot(a, b, preferred_element_type=F32)


def _dot_nt(a, b):
    return lax.dot_general(a, b, (((1,), (1,)), ((), ())), preferred_element_type=F32)


def _dot_tn(a, b):
    return lax.dot_general(a, b, (((0,), (0,)), ((), ())), preferred_element_type=F32)


def _split(x):
    hi = x.astype(BF16)
    lo = (x - hi.astype(F32)).astype(BF16)
    return hi, lo


def _dot_exact_lhs(sel2, x):
    hi, lo = _split(x)
    return _dot(sel2, jnp.concatenate([hi, lo], axis=0))


def _sigmoid(x):
    return 1.0 / (1.0 + jnp.exp(-x))


def _silu(x):
    return x * _sigmoid(x)


def _iota(shape, dim):
    return lax.broadcasted_iota(jnp.int32, shape, dim)


def _full_spec(shape):
    return pl.BlockSpec(shape, lambda *_: (0,) * len(shape))


def _mm_kernel(a_ref, w_ref, o_ref):
    o_ref[...] = _dot(a_ref[...], w_ref[...]).astype(o_ref.dtype)


def _matmul(a, w, layer, out_dtype, tm, tn):
    m, k = a.shape
    n = w.shape[2]
    return pl.pallas_call(
        _mm_kernel,
        grid=(n // tn, m // tm),
        in_specs=[pl.BlockSpec((tm, k), lambda j, i: (i, 0)),
                  pl.BlockSpec((None, k, tn), lambda j, i: (layer, 0, j))],
        out_specs=pl.BlockSpec((tm, tn), lambda j, i: (i, j)),
        out_shape=jax.ShapeDtypeStruct((m, n), out_dtype),
        compiler_params=_cparams(("arbitrary", "arbitrary")),
        name="matmul",
    )(a, w)


def _ln_rows(y, g, b):
    mu = jnp.mean(y, axis=-1, keepdims=True)
    d = y - mu
    var = jnp.mean(d * d, axis=-1, keepdims=True)
    return d * lax.rsqrt(var + LN_EPS) * g + b


def _ln_kernel(x_ref, g_ref, b_ref, o_ref):
    o_ref[...] = _ln_rows(x_ref[...], g_ref[...], b_ref[...]).astype(o_ref.dtype)


def _layer_norm(x, g, b, out_dtype, tm):
    m, d = x.shape
    return pl.pallas_call(
        _ln_kernel,
        grid=(m // tm,),
        in_specs=[pl.BlockSpec((tm, d), lambda i: (i, 0)),
                  pl.BlockSpec((1, d), lambda i: (0, 0)),
                  pl.BlockSpec((1, d), lambda i: (0, 0))],
        out_specs=pl.BlockSpec((tm, d), lambda i: (i, 0)),
        out_shape=jax.ShapeDtypeStruct((m, d), out_dtype),
        compiler_params=_cparams(("arbitrary",)),
        name="layer_norm",
    )(x, g.reshape(1, d), b.reshape(1, d))


def _proj_ln_kernel(a0_ref, a1_ref, a2_ref, w_ref, x_ref, g_ref, b_ref, of_ref, ob_ref, *, alpha, bounds):
    kk = pl.program_id(1)
    n0, n1, nk = bounds
    tm, d = of_ref.shape
    rows = [slice(r, r + ROW_SLICE) for r in range(0, tm, ROW_SLICE)]
    cols = [slice(c, c + COL_SLICE) for c in range(0, d, COL_SLICE)]

    @pl.when(kk == 0)
    def _():
        of_ref[...] = jnp.zeros_like(of_ref)

    def accumulate(a_ref):
        a = a_ref[...]
        for sl in cols:
            of_ref[:, sl] += _dot(a, w_ref[:, sl])

    pl.when(kk < n0)(lambda: accumulate(a0_ref))
    pl.when((kk >= n0) & (kk < n1))(lambda: accumulate(a1_ref))
    pl.when(kk >= n1)(lambda: accumulate(a2_ref))

    @pl.when(kk == nk - 1)
    def _():
        for sl in rows:
            y = _ln_rows(alpha * x_ref[sl, :] + of_ref[sl, :], g_ref[...], b_ref[...])
            of_ref[sl, :] = y
            ob_ref[sl, :] = y.astype(BF16)


def _proj_ln(a0, a1, a2, w, layer, x, g, b, alpha, tm, tk):
    m = a0.shape[0]
    d = w.shape[2]
    n0 = a0.shape[1] // tk
    n1 = n0 + a1.shape[1] // tk
    nk = n1 + a2.shape[1] // tk
    assert nk * tk == w.shape[1]
    return pl.pallas_call(
        functools.partial(_proj_ln_kernel, alpha=alpha, bounds=(n0, n1, nk)),
        grid=(m // tm, nk),
        in_specs=[pl.BlockSpec((tm, tk), lambda i, kk: (i, jnp.minimum(kk, n0 - 1))),
                  pl.BlockSpec((tm, tk), lambda i, kk: (i, jnp.clip(kk - n0, 0, n1 - n0 - 1))),
                  pl.BlockSpec((tm, tk), lambda i, kk: (i, jnp.clip(kk - n1, 0, nk - n1 - 1))),
                  pl.BlockSpec((None, tk, d), lambda i, kk: (layer, kk, 0)),
                  pl.BlockSpec((tm, d), lambda i, kk: (i, 0)),
                  pl.BlockSpec((1, d), lambda i, kk: (0, 0)),
                  pl.BlockSpec((1, d), lambda i, kk: (0, 0))],
        out_specs=[pl.BlockSpec((tm, d), lambda i, kk: (i, 0)),
                   pl.BlockSpec((tm, d), lambda i, kk: (i, 0))],
        out_shape=[jax.ShapeDtypeStruct((m, d), F32), jax.ShapeDtypeStruct((m, d), BF16)],
        compiler_params=_cparams(("arbitrary", "arbitrary")),
        name="proj_ln",
    )(a0, a1, a2, w, x, g.reshape(1, d), b.reshape(1, d))


_LEVELS = (32, 16, 8, 4, 2, 1)
_N_SEL = len(_LEVELS) + 2


def _range_selectors():
    i = np.arange(CHUNK)[:, None]
    t = np.arange(CHUNK)[None, :]
    blocks = [t <= i]
    for s in _LEVELS:
        m = (i & ~(2 * s - 1)) | (s - 1)
        later = (i & s) != 0
        blocks.append(np.where(later, (t > m) & (t <= i), (t > i) & (t <= m)))
    blocks.append(t > i)
    sel = np.concatenate(blocks, axis=0).astype(np.float32)
    return jnp.asarray(np.concatenate([sel, sel], axis=1), dtype=BF16)


def _level_masks():
    i = np.arange(CHUNK)[:, None]
    j = np.arange(CHUNK)[None, :]
    masks = [i == j]
    for s in _LEVELS:
        sh = s.bit_length()
        masks.append(((i >> sh) == (j >> sh)) & ((i & s) != 0) & ((j & s) == 0))
    return jnp.asarray(np.stack(masks).astype(np.float32))


def _diag_gated_block(q_ref, k_ref, v_ref, la_ref, e_ref, rsel_ref, mask_ref, state_t, nchunks, group, emit):
    dk = q_ref.shape[1]
    row = _iota((CHUNK, dk), 0)
    laters = [(row & s) != 0 for s in _LEVELS]
    rsel = rsel_ref[...]
    span = _N_SEL * CHUNK
    for g0 in range(0, nchunks, group):
        cs = list(range(g0, min(g0 + group, nchunks)))
        for c in cs:
            e_ref[c * span:(c + 1) * span, :] = _dot_exact_lhs(rsel, la_ref[c * CHUNK:(c + 1) * CHUNK, :])
        qs = [q_ref[c * CHUNK:(c + 1) * CHUNK, :] for c in cs]
        ks = [k_ref[c * CHUNK:(c + 1) * CHUNK, :] for c in cs]
        vbs = [v_ref[c * CHUNK:(c + 1) * CHUNK, :].astype(BF16) for c in cs]
        scores = [_dot_nt(q.astype(BF16), k.astype(BF16)) * mask_ref[0] for q, k in zip(qs, ks)]
        for n in range(len(_LEVELS)):
            zs = [(jnp.where(laters[n], q, k)
                   * jnp.exp(e_ref[c * span + (n + 1) * CHUNK:c * span + (n + 2) * CHUNK, :])).astype(BF16)
                  for c, q, k in zip(cs, qs, ks)]
            scores = [sc + _dot_nt(z, z) * mask_ref[n + 1] for sc, z in zip(scores, zs)]
        o_intra = [_dot(sc.astype(BF16), vb) for sc, vb in zip(scores, vbs)]
        q_dec = [(q * jnp.exp(e_ref[c * span:c * span + CHUNK, :])).astype(BF16) for c, q in zip(cs, qs)]
        k_dec = [(k * jnp.exp(e_ref[c * span + 7 * CHUNK:(c + 1) * span, :])).astype(BF16) for c, k in zip(cs, ks)]
        ds_t = [_dot_tn(vb, kd) for vb, kd in zip(vbs, k_dec)]
        for n, c in enumerate(cs):
            emit(c, o_intra[n] + _dot_nt(q_dec[n], state_t.astype(BF16)))
            decay = jnp.exp(e_ref[c * span + CHUNK - 1:c * span + CHUNK, :])
            state_t = state_t * decay + ds_t[n]
    return state_t


def _rms_gate(o, g, gate):
    y = o * lax.rsqrt(jnp.mean(o * o, axis=-1, keepdims=True) + RMS_EPS)
    return y * g * _silu(gate)


def _hgrn_kernel(q_ref, f_ref, i_ref, g_ref, lb_ref, ng_ref, rsel_ref, mask_ref, o_ref,
                 st_ref, qs_ref, ks_ref, la_ref, e_ref, *, nchunks, group):
    @pl.when(pl.program_id(2) == 0)
    def _():
        st_ref[...] = jnp.zeros_like(st_ref)

    lb = lb_ref[0]
    f = lb + (1.0 - lb) * _sigmoid(f_ref[...])
    qs_ref[...] = _silu(q_ref[...])
    ks_ref[...] = 1.0 - f
    la_ref[...] = jnp.log(f)
    ng = ng_ref[...]

    def emit(c, o):
        sl = slice(c * CHUNK, (c + 1) * CHUNK)
        o_ref[sl, :] = _rms_gate(o, ng, g_ref[sl, :]).astype(o_ref.dtype)

    st_ref[...] = _diag_gated_block(qs_ref, ks_ref, i_ref, la_ref, e_ref, rsel_ref, mask_ref, st_ref[...],
                                    nchunks, group, emit)


def _hgrn(proj, lb, norm_g, rsel, masks, bsz, seq, tblk, group):
    nt = seq // tblk

    def col(base):
        return pl.BlockSpec((tblk, LANES), lambda b, h, t: (b * nt + t, base + h))

    return pl.pallas_call(
        functools.partial(_hgrn_kernel, nchunks=tblk // CHUNK, group=group),
        grid=(bsz, A_HEADS, nt),
        in_specs=[col(COL_AQ), col(COL_AF), col(COL_AI), col(COL_AG),
                  pl.BlockSpec((1, 1, LANES), lambda b, h, t: (h, 0, 0)),
                  _full_spec((1, A_DV)), _full_spec(rsel.shape), _full_spec(masks.shape)],
        out_specs=pl.BlockSpec((tblk, A_DV), lambda b, h, t: (b * nt + t, h)),
        out_shape=jax.ShapeDtypeStruct((bsz * seq, A_HEADS * A_DV), BF16),
        scratch_shapes=[pltpu.VMEM((A_DV, HEAD_DK), F32), pltpu.VMEM((tblk, HEAD_DK), F32),
                        pltpu.VMEM((tblk, HEAD_DK), F32), pltpu.VMEM((tblk, HEAD_DK), F32),
                        pltpu.VMEM((tblk * _N_SEL, HEAD_DK), F32)],
        compiler_params=_cparams(("arbitrary", "arbitrary", "arbitrary")),
        name="hgrn2",
    )(proj, proj, proj, proj, lb, norm_g.reshape(1, A_DV), rsel, masks)


def _gla_kernel(q_ref, k_ref, v_ref, r_ref, misc_ref, wup_ref, bup_ref, ng_ref, rsel_ref, mask_ref, o_ref,
                st_ref, qs_ref, la_ref, e_ref, *, nchunks, group):
    @pl.when(pl.program_id(2) == 0)
    def _():
        st_ref[...] = jnp.zeros_like(st_ref)

    z = _dot(misc_ref[...].astype(BF16), wup_ref[0]) + bup_ref[0]
    la_ref[...] = (jnp.minimum(z, 0.0) - jnp.log(1.0 + jnp.exp(-jnp.abs(z)))) * (1.0 / C_TAU)
    qs_ref[...] = q_ref[...] * (HEAD_DK ** -0.5)
    ng = ng_ref[...]

    def emit(c, o):
        sl = slice(c * CHUNK, (c + 1) * CHUNK)
        o_ref[sl, :] = _rms_gate(o, ng, r_ref[sl, :]).astype(o_ref.dtype)

    st_ref[...] = _diag_gated_block(qs_ref, k_ref, v_ref, la_ref, e_ref, rsel_ref, mask_ref, st_ref[...],
                                    nchunks, group, emit)


def _gla(proj, misc, w_up_pad, b_up, norm_g, rsel, masks, bsz, seq, tblk, group):
    nt = seq // tblk

    def col(base):
        return pl.BlockSpec((tblk, LANES), lambda b, h, t: (b * nt + t, base + h))

    def col2(base):
        return pl.BlockSpec((tblk, C_DV), lambda b, h, t: (b * nt + t, base // 2 + h))

    return pl.pallas_call(
        functools.partial(_gla_kernel, nchunks=tblk // CHUNK, group=group),
        grid=(bsz, C_HEADS, nt),
        in_specs=[col(COL_CQ), col(COL_CK), col2(COL_CV), col2(COL_CR),
                  pl.BlockSpec((tblk, LANES), lambda b, h, t: (b * nt + t, 0)),
                  pl.BlockSpec((1, LANES, LANES), lambda b, h, t: (h, 0, 0)),
                  pl.BlockSpec((1, 1, LANES), lambda b, h, t: (h, 0, 0)),
                  _full_spec((1, C_DV)), _full_spec(rsel.shape), _full_spec(masks.shape)],
        out_specs=pl.BlockSpec((tblk, C_DV), lambda b, h, t: (b * nt + t, h)),
        out_shape=jax.ShapeDtypeStruct((bsz * seq, C_HEADS * C_DV), BF16),
        scratch_shapes=[pltpu.VMEM((C_DV, HEAD_DK), F32), pltpu.VMEM((tblk, HEAD_DK), F32),
                        pltpu.VMEM((tblk, HEAD_DK), F32), pltpu.VMEM((tblk * _N_SEL, HEAD_DK), F32)],
        compiler_params=_cparams(("arbitrary", "arbitrary", "arbitrary")),
        name="gla",
    )(proj, proj, proj, proj, misc, w_up_pad, b_up, norm_g.reshape(1, C_DV), rsel, masks)


def _times_p(lefts, p):
    n = p.shape[0]
    ph, pl_ = _split(p)
    halves = [_split(x) for x in lefts]
    by_hi = _dot(jnp.concatenate([h for pair in halves for h in pair], axis=0), ph)
    by_lo = _dot(jnp.concatenate([hi for hi, _ in halves], axis=0), pl_)
    return [by_hi[2 * i * n:(2 * i + 1) * n] + by_hi[(2 * i + 1) * n:(2 * i + 2) * n] + by_lo[i * n:(i + 1) * n]
            for i in range(len(lefts))]


def _unit_lower_inverses(neg_ms):
    shape = neg_ms[0].shape
    eye = jnp.where(_iota(shape, 0) == _iota(shape, 1), 1.0, 0.0)
    ts = [eye + m for m in neg_ms]
    ps = [_times_p([m], m)[0] for m in neg_ms]
    for _ in range(4):
        prods = [_times_p([t, p], p) for t, p in zip(ts, ps)]
        ts = [t + tp for t, (tp, _) in zip(ts, prods)]
        ps = [pp for _, pp in prods]
    return [t + _times_p([t], p)[0] for t, p in zip(ts, ps)]


def _gdn_block(buf_ref, state, nchunks, group, emit):
    ri = _iota((CHUNK, CHUNK), 0)
    ci = _iota((CHUNK, CHUNK), 1)
    incl = ri >= ci
    strict = ri > ci
    tri = jnp.where(_iota((CHUNK, 2 * CHUNK), 0) >= (_iota((CHUNK, 2 * CHUNK), 1) & (CHUNK - 1)),
                    1.0, 0.0).astype(BF16)
    upper = jnp.where(strict, 1.0, 0.0)
    for g0 in range(0, nchunks, group):
        cs = list(range(g0, min(g0 + group, nchunks)))
        sls = [slice(c * CHUNK, (c + 1) * CHUNK) for c in cs]
        qs = [buf_ref[3, sl, :] for sl in sls]
        ks = [buf_ref[4, sl, :] for sl in sls]
        vs = [buf_ref[5, sl, :] for sl in sls]
        betas = [buf_ref[6, sl, :] for sl in sls]
        gs = [buf_ref[7, sl, :] for sl in sls]
        gcs = [_dot_exact_lhs(tri, g) for g in gs]
        dmats = [_dot_exact_lhs(tri, g[:, :CHUNK] * upper) for g in gs]
        gammas = [jnp.where(incl, jnp.exp(jnp.where(incl, dm, 0.0)), 0.0) for dm in dmats]
        qbs = [q.astype(BF16) for q in qs]
        kbs = [k.astype(BF16) for k in ks]
        kks = [_dot_nt(kb, kb) for kb in kbs]
        ts = _unit_lower_inverses([-jnp.where(strict, beta[:, :CHUNK] * kk * gamma, 0.0)
                                   for beta, kk, gamma in zip(betas, kks, gammas)])
        egcs = [jnp.exp(gc) for gc in gcs]
        rhss = [jnp.concatenate([k * (beta * egc), v * beta], axis=1).astype(BF16)
                for k, v, beta, egc in zip(ks, vs, betas, egcs)]
        wus = []
        for t, rhs in zip(ts, rhss):
            both = _dot(jnp.concatenate(_split(t), axis=0), rhs)
            wus.append((both[:CHUNK] + both[CHUNK:]).astype(BF16))
        attns = [jnp.where(incl, _dot_nt(qb, kb) * gamma, 0.0).astype(BF16)
                 for qb, kb, gamma in zip(qbs, kbs, gammas)]
        gls = [gc[CHUNK - 1:CHUNK, :] for gc in gcs]
        k_decs = [(k * jnp.exp(gl - gc)).astype(BF16) for k, gl, gc in zip(ks, gls, gcs)]
        kwus = [_dot_tn(kd, wu) for kd, wu in zip(k_decs, wus)]
        awus = [_dot(a, wu) for a, wu in zip(attns, wus)]
        q_effs = [(q * egc - awu[:, :HEAD_DK]).astype(BF16) for q, egc, awu in zip(qs, egcs, awus)]
        for n, c in enumerate(cs):
            st_b = state.astype(BF16)
            emit(c, _dot(q_effs[n], st_b) + awus[n][:, HEAD_DK:])
            state = (state * jnp.exp(gls[n]) - _dot(kwus[n][:, :HEAD_DK].astype(BF16), st_b)
                     + kwus[n][:, HEAD_DK:])
    return state


def _l2n(x):
    return x * lax.rsqrt(jnp.sum(x * x, axis=-1, keepdims=True) + RMS_EPS)


def _gdn_kernel(q_ref, k_ref, v_ref, z_ref, misc_ref, cw_ref, al_ref, dt_ref, ng_ref, o_ref,
                st_ref, buf_ref, *, nchunks, tblk, group):
    h = pl.program_id(1)

    @pl.when(pl.program_id(2) == 0)
    def _():
        st_ref[...] = jnp.zeros_like(st_ref)
        buf_ref[...] = jnp.zeros_like(buf_ref)

    conv = []
    for n, ref in enumerate((q_ref, k_ref, v_ref)):
        buf_ref[n, 8:, :] = ref[...]
        acc = jnp.zeros((tblk, LANES), F32)
        for j in range(B_CONV):
            acc = acc + cw_ref[n, j:j + 1, :] * buf_ref[n, 8 - (B_CONV - 1) + j:8 - (B_CONV - 1) + j + tblk, :]
        buf_ref[n, 0:8, :] = buf_ref[n, tblk:tblk + 8, :]
        conv.append(_silu(acc))
    buf_ref[3, 0:tblk, :] = _l2n(conv[0]) * (HEAD_DK ** -0.5)
    buf_ref[4, 0:tblk, :] = _l2n(conv[1])
    buf_ref[5, 0:tblk, :] = conv[2]

    lane_r = _iota((LANES, LANES), 0)
    sel_beta = jnp.where(lane_r == MISC_BETA + h, 1.0, 0.0).astype(BF16)
    sel_dec = jnp.where(lane_r == MISC_DECAY + h, 1.0, 0.0).astype(BF16)
    mh, ml = _split(misc_ref[...])
    beta_logit = _dot(mh, sel_beta) + _dot(ml, sel_beta)
    dec_in = _dot(mh, sel_dec) + _dot(ml, sel_dec) + dt_ref[...]
    softplus = jnp.maximum(dec_in, 0.0) + jnp.log(1.0 + jnp.exp(-jnp.abs(dec_in)))
    buf_ref[6, 0:tblk, :] = _sigmoid(beta_logit)
    buf_ref[7, 0:tblk, :] = -jnp.exp(al_ref[...]) * softplus

    ng = ng_ref[...]

    def emit(c, o):
        sl = slice(c * CHUNK, (c + 1) * CHUNK)
        o_ref[sl, :] = _rms_gate(o, ng, z_ref[sl, :]).astype(o_ref.dtype)

    st_ref[...] = _gdn_block(buf_ref, st_ref[...], nchunks, group, emit)


def _gdn(proj, misc, conv_w, a_log, dt_bias, norm_g, bsz, seq, tblk, group):
    nt = seq // tblk

    def col(base):
        return pl.BlockSpec((tblk, LANES), lambda b, h, t: (b * nt + t, base + h))

    def per_head(shape):
        return pl.BlockSpec((None,) + shape, lambda b, h, t: (h,) + (0,) * len(shape))

    return pl.pallas_call(
        functools.partial(_gdn_kernel, nchunks=tblk // CHUNK, tblk=tblk, group=group),
        grid=(bsz, B_HEADS, nt),
        in_specs=[col(COL_BQ), col(COL_BK), col(COL_BV), col(COL_BZ),
                  pl.BlockSpec((tblk, LANES), lambda b, h, t: (b * nt + t, 0)),
                  per_head((3, B_CONV, LANES)), per_head((1, LANES)), per_head((1, LANES)),
                  _full_spec((1, B_DV))],
        out_specs=pl.BlockSpec((tblk, B_DV), lambda b, h, t: (b * nt + t, h)),
        out_shape=jax.ShapeDtypeStruct((bsz * seq, B_HEADS * B_DV), BF16),
        scratch_shapes=[pltpu.VMEM((HEAD_DK, B_DV), F32), pltpu.VMEM((8, tblk + 8, LANES), F32)],
        compiler_params=_cparams(("arbitrary", "arbitrary", "arbitrary")),
        name="gdn",
    )(proj, proj, proj, proj, misc, conv_w, a_log, dt_bias, norm_g.reshape(1, B_DV))


def _xattn_kernel(xf_ref, wq_ref, kv_ref, wo_ref, g_ref, b_ref, of_ref, ob_ref, *, alpha):
    q = _dot(xf_ref[...].astype(BF16), wq_ref[...])
    width = X_HEADS * X_HEAD_DIM
    outs = []
    for hd in range(X_HEADS):
        lo = hd * X_HEAD_DIM
        qh = q[:, lo:lo + X_HEAD_DIM].astype(BF16)
        kh = kv_ref[:, lo:lo + X_HEAD_DIM]
        vh = kv_ref[:, width + lo:width + lo + X_HEAD_DIM]
        s = _dot_nt(qh, kh) * (X_HEAD_DIM ** -0.5)
        s = s - jnp.max(s, axis=-1, keepdims=True)
        p = jnp.exp(s)
        p = p / jnp.sum(p, axis=-1, keepdims=True)
        outs.append(_dot(p.astype(BF16), vh))
    o = jnp.concatenate(outs, axis=1).astype(BF16)
    tq, d = of_ref.shape
    for c in range(0, d, COL_SLICE):
        of_ref[:, c:c + COL_SLICE] = _dot(o, wo_ref[:, c:c + COL_SLICE])
    for r in range(0, tq, ROW_SLICE):
        sl = slice(r, r + ROW_SLICE)
        y = _ln_rows(alpha * xf_ref[sl, :] + of_ref[sl, :], g_ref[...], b_ref[...])
        of_ref[sl, :] = y
        ob_ref[sl, :] = y.astype(BF16)


def _xattn(xf, wq, kv, wo, layer, g, b, alpha, bsz, seq, tq):
    d = xf.shape[1]
    nt = seq // tq
    width = X_HEADS * X_HEAD_DIM
    n_mem = kv.shape[1]
    return pl.pallas_call(
        functools.partial(_xattn_kernel, alpha=alpha),
        grid=(bsz, nt),
        in_specs=[pl.BlockSpec((tq, d), lambda bb, t: (bb * nt + t, 0)),
                  pl.BlockSpec((None, d, width), lambda bb, t: (layer, 0, 0), pipeline_mode=pl.Buffered(1)),
                  pl.BlockSpec((None, n_mem, 2 * width), lambda bb, t: (bb, 0, 0)),
                  pl.BlockSpec((None, width, d), lambda bb, t: (layer, 0, 0), pipeline_mode=pl.Buffered(1)),
                  pl.BlockSpec((1, d), lambda bb, t: (0, 0)),
                  pl.BlockSpec((1, d), lambda bb, t: (0, 0))],
        out_specs=[pl.BlockSpec((tq, d), lambda bb, t: (bb * nt + t, 0)),
                   pl.BlockSpec((tq, d), lambda bb, t: (bb * nt + t, 0))],
        out_shape=[jax.ShapeDtypeStruct((bsz * seq, d), F32), jax.ShapeDtypeStruct((bsz * seq, d), BF16)],
        compiler_params=_cparams(("arbitrary", "arbitrary")),
        name="xattn",
    )(xf, wq, kv, wo, g.reshape(1, d), b.reshape(1, d))


def _router_kernel(xb_ref, w_ref, b_ref, o_ref):
    logits = _dot(xb_ref[...], w_ref[...]) + b_ref[...]
    lane = _iota(logits.shape, 1).astype(F32)
    neg = jnp.float32(-3.0e38)
    picked = []
    vals = []
    work = logits
    for _ in range(TOP_K):
        mx = jnp.max(work, axis=-1, keepdims=True)
        idx = jnp.min(jnp.where(work == mx, lane, float(LANES)), axis=-1, keepdims=True)
        sel = lane == idx
        picked.append(sel)
        vals.append(mx)
        work = jnp.where(sel, neg, work)
    es = [jnp.exp(vv - vals[0]) for vv in vals]
    inv = 1.0 / (es[0] + es[1] + es[2] + es[3])
    comb = jnp.zeros_like(logits)
    for sel, e in zip(picked, es):
        comb = comb + jnp.where(sel, e * inv, 0.0)
    o_ref[...] = comb


def _router(xb, w_pad, b_pad, tm):
    m, d = xb.shape
    return pl.pallas_call(
        _router_kernel,
        grid=(m // tm,),
        in_specs=[pl.BlockSpec((tm, d), lambda i: (i, 0)),
                  pl.BlockSpec((d, LANES), lambda i: (0, 0)),
                  pl.BlockSpec((1, LANES), lambda i: (0, 0))],
        out_specs=pl.BlockSpec((tm, LANES), lambda i: (i, 0)),
        out_shape=jax.ShapeDtypeStruct((m, LANES), F32),
        compiler_params=_cparams(("arbitrary",)),
        name="router",
    )(xb, w_pad, b_pad)


def _moe_up_kernel(xb_ref, comb_ref, wg_ref, bg_ref, wu_ref, bu_ref, h_ref):
    e = pl.program_id(1)
    sel = jnp.where(_iota((LANES, EXPERT_FF), 0) == e, 1.0, 0.0).astype(BF16)
    xb = xb_ref[...]
    gate = jnp.minimum(_dot(xb, wg_ref[...].astype(BF16)) + bg_ref[...], SWIGLU_LIMIT)
    lin = jnp.clip(_dot(xb, wu_ref[...].astype(BF16)) + bu_ref[...], -SWIGLU_LIMIT, SWIGLU_LIMIT)
    hid = (lin + 1.0) * gate * _sigmoid(SWIGLU_ALPHA * gate)
    ch, cl = _split(comb_ref[...])
    weight = _dot(ch, sel) + _dot(cl, sel)
    h_ref[...] = (hid * weight).astype(BF16)


def _moe_up(xb, comb, wg, bg, wu, bu, layer, ne, tm):
    m, d = xb.shape
    ff = wg.shape[2]
    base = layer * ne
    return pl.pallas_call(
        _moe_up_kernel,
        grid=(m // tm, ne),
        in_specs=[pl.BlockSpec((tm, d), lambda i, e: (i, 0)),
                  pl.BlockSpec((tm, LANES), lambda i, e: (i, 0)),
                  pl.BlockSpec((None, d, ff), lambda i, e: (base + e, 0, 0)),
                  pl.BlockSpec((None, 1, ff), lambda i, e: (base + e, 0, 0)),
                  pl.BlockSpec((None, d, ff), lambda i, e: (base + e, 0, 0)),
                  pl.BlockSpec((None, 1, ff), lambda i, e: (base + e, 0, 0))],
        out_specs=pl.BlockSpec((tm, ff), lambda i, e: (i, e)),
        out_shape=jax.ShapeDtypeStruct((m, ne * ff), BF16),
        compiler_params=_cparams(("arbitrary", "arbitrary")),
        name="moe_up",
    )(xb, comb, wg, bg, wu, bu)


def _moe_down_kernel(h_ref, wd_ref, comb_ref, bd_ref, xf_ref, g_ref, b_ref, of_ref, ob_ref, *, alpha, nk):
    kk = pl.program_id(1)
    tm, d = of_ref.shape

    @pl.when(kk == 0)
    def _():
        of_ref[...] = jnp.zeros_like(of_ref)

    h = h_ref[...]
    for c in range(0, d, COL_SLICE):
        of_ref[:, c:c + COL_SLICE] += _dot(h, wd_ref[:, c:c + COL_SLICE])

    @pl.when(kk == nk - 1)
    def _():
        for r in range(0, tm, ROW_SLICE):
            sl = slice(r, r + ROW_SLICE)
            y = of_ref[sl, :] + _dot(comb_ref[sl, :].astype(BF16), bd_ref[...])
            y = _ln_rows(alpha * xf_ref[sl, :] + y, g_ref[...], b_ref[...])
            of_ref[sl, :] = y
            ob_ref[sl, :] = y.astype(BF16)


def _moe_down(h, wd, layer, comb, bd_pad, xf, g, b, alpha, tm, tk):
    m, d = xf.shape
    nk = h.shape[1] // tk
    once = pl.Buffered(1)
    return pl.pallas_call(
        functools.partial(_moe_down_kernel, alpha=alpha, nk=nk),
        grid=(m // tm, nk),
        in_specs=[pl.BlockSpec((tm, tk), lambda i, kk: (i, kk)),
                  pl.BlockSpec((None, tk, d), lambda i, kk: (layer, kk, 0)),
                  pl.BlockSpec((tm, LANES), lambda i, kk: (i, 0), pipeline_mode=once),
                  pl.BlockSpec((LANES, d), lambda i, kk: (0, 0), pipeline_mode=once),
                  pl.BlockSpec((tm, d), lambda i, kk: (i, 0), pipeline_mode=once),
                  pl.BlockSpec((1, d), lambda i, kk: (0, 0)),
                  pl.BlockSpec((1, d), lambda i, kk: (0, 0))],
        out_specs=[pl.BlockSpec((tm, d), lambda i, kk: (i, 0)),
                   pl.BlockSpec((tm, d), lambda i, kk: (i, 0))],
        out_shape=[jax.ShapeDtypeStruct((m, d), F32), jax.ShapeDtypeStruct((m, d), BF16)],
        compiler_params=_cparams(("arbitrary", "arbitrary")),
        name="moe_down",
    )(h, wd, comb, bd_pad, xf, g.reshape(1, d), b.reshape(1, d))


def _pick(n, prefs):
    for p in prefs:
        if n % p == 0:
            return p
    raise ValueError(f"no tile in {prefs} divides {n}")


def kernel(x, mem, w_in, hgrn_lb_raw, hgrn_norm_g, gdn_conv_w, gdn_a_log, gdn_dt_bias, gdn_norm_g, gla_w_up, gla_b_up, gla_norm_g, w_out, ln_mix_g, ln_mix_b, mem_ln_g, mem_ln_b, xattn_wq, xattn_wkv, xattn_wo, ln_xattn_g, ln_xattn_b, w_router, b_router, w_gate, b_gate, w_up, b_up, w_down, b_down, ln_ffn_g, ln_ffn_b):
    bsz, seq, d = x.shape
    depth = w_in.shape[0]
    n_mem = mem.shape[1]
    ntok = bsz * seq
    alpha = float((2 * depth) ** 0.25)
    assert seq % CHUNK == 0 and d % LANES == 0

    tblk = _pick(seq, (512, 256, 128, 64))
    group = min(8, tblk // CHUNK)
    tblk_gdn = _pick(seq, (1024, 512, 256, 128, 64))
    tm_big = _pick(ntok, (512, 256, 128, 64))
    tm_mid = _pick(ntok, (256, 128, 64))
    tm_up = _pick(ntok, (1024, 512, 256, 128, 64))

    a_w, b_w, c_w = A_HEADS * HEAD_DK, B_HEADS * HEAD_DK, C_HEADS * HEAD_DK
    o_bb = 4 * a_w + (2 * b_w + B_HEADS * B_DV) + B_HEADS * B_DV
    o_cq = o_bb + 2 * B_HEADS
    o_clr = o_cq + 2 * c_w + 2 * C_HEADS * C_DV
    n_cols = o_clr + C_RANK
    assert w_in.shape[2] == n_cols
    assert o_bb == MAIN_COLS
    w_main = w_in[:, :, :o_bb].astype(BF16)
    w_c = w_in[:, :, o_cq:o_clr].astype(BF16)
    w_misc = jnp.concatenate([w_in[:, :, o_bb:o_cq], w_in[:, :, o_clr:],
                              jnp.zeros((depth, d, LANES - 2 * B_HEADS - C_RANK), w_in.dtype)],
                             axis=2).astype(BF16)
    w_out_b = w_out.astype(BF16)
    wq_b, wkv_b, wo_b = xattn_wq.astype(BF16), xattn_wkv.astype(BF16), xattn_wo.astype(BF16)
    wr_p = jnp.concatenate([w_router, jnp.zeros((depth, d, LANES - N_EXPERTS), F32)], axis=2).astype(BF16)
    br_p = jnp.concatenate([b_router, jnp.full((depth, LANES - N_EXPERTS), -1.0e30, F32)], axis=1)
    ne, ff = w_gate.shape[1], w_gate.shape[3]
    wg_r = w_gate.reshape(depth * ne, d, ff)
    wu_r = w_up.reshape(depth * ne, d, ff)
    wd_b = w_down.astype(BF16).reshape(depth, ne * ff, d)
    bg_r = b_gate.reshape(depth * ne, 1, ff)
    bu_r = b_up.reshape(depth * ne, 1, ff)
    bd_p = jnp.concatenate([b_down, jnp.zeros((depth, LANES - N_EXPERTS, d), F32)], axis=1).astype(BF16)
    gla_wup_p = jnp.zeros((depth, LANES, c_w), F32).at[:, MISC_CODE:MISC_CODE + C_RANK, :].set(gla_w_up)
    gla_wup_p = gla_wup_p.reshape(depth, LANES, C_HEADS, HEAD_DK).transpose(0, 2, 1, 3).astype(BF16)
    gla_bup_r = gla_b_up.reshape(depth, C_HEADS, 1, HEAD_DK)
    conv_r = gdn_conv_w.reshape(depth, B_CONV, 3, B_HEADS, HEAD_DK).transpose(0, 3, 2, 1, 4)
    a_log_r = jnp.broadcast_to(gdn_a_log[:, :, None, None], (depth, B_HEADS, 1, LANES))
    dt_r = jnp.broadcast_to(gdn_dt_bias[:, :, None, None], (depth, B_HEADS, 1, LANES))
    lb_all = jnp.cumsum(jax.nn.softmax(hgrn_lb_raw.astype(F32), axis=0), axis=0)
    lb_r = (lb_all - lb_all[0:1]).reshape(depth, A_HEADS, 1, HEAD_DK)
    rsel = _range_selectors()
    masks = _level_masks()

    tm_mem = _pick(bsz * n_mem, (256, 128, 64, 8))
    mem_n = _layer_norm(mem.reshape(bsz * n_mem, d), mem_ln_g, mem_ln_b, BF16, tm_mem)

    xf = x.reshape(ntok, d)
    xb = xf.astype(BF16)
    for l in range(depth):
        proj = _matmul(xb, w_main, l, F32, tm_big, 1280)
        proj_c = _matmul(xb, w_c, l, F32, tm_big, 1536)
        misc = _matmul(xb, w_misc, l, F32, tm_big, LANES)
        oa = _hgrn(proj, lb_r[l], hgrn_norm_g[l], rsel, masks, bsz, seq, tblk, group)
        ob = _gdn(proj, misc, conv_r[l], a_log_r[l], dt_r[l], gdn_norm_g[l], bsz, seq, tblk_gdn, group)
        oc = _gla(proj_c, misc, gla_wup_p[l], gla_bup_r[l], gla_norm_g[l], rsel, masks, bsz, seq, tblk, group)
        xf, xb = _proj_ln(oa, ob, oc, w_out_b, l, xf, ln_mix_g[l], ln_mix_b[l], alpha, tm_big, 512)

        kv = _matmul(mem_n, wkv_b, l, BF16, tm_mem, 2 * X_HEADS * X_HEAD_DIM)
        xf, xb = _xattn(xf, wq_b, kv.reshape(bsz, n_mem, -1), wo_b, l, ln_xattn_g[l], ln_xattn_b[l],
                        alpha, bsz, seq, _pick(seq, (512, 256, 128, 64)))

        comb = _router(xb, wr_p[l], br_p[l, None, :], tm_big)
        hid = _moe_up(xb, comb, wg_r, bg_r, wu_r, bu_r, l, ne, tm_up)
        xf, xb = _moe_down(hid, wd_b, l, comb, bd_p[l], xf, ln_ffn_g[l], ln_ffn_b[l], alpha, tm_big, 512)
    return xf.reshape(bsz, seq, d)
```

```python
import functools

import numpy as np
import jax
import jax.numpy as jnp
from jax import lax
from jax.experimental import pallas as pl
from jax.experimental.pallas import tpu as pltpu

F32 = jnp.float32
BF16 = jnp.bfloat16

LANES = 128
CHUNK = 64
VMEM_LIMIT = 62 * 1024 * 1024
ROW_SLICE = 256
COL_SLICE = 1024

A_HEADS, B_HEADS, C_HEADS = 8, 12, 6
HEAD_DK = 128
A_DV, B_DV, C_DV = 128, 128, 256
B_CONV = 4
GDN_HEADS_PER_STEP = 4
C_RANK = 16
C_TAU = 16.0
X_HEADS, X_HEAD_DIM = 4, 128
N_EXPERTS, TOP_K, EXPERT_FF = 32, 4, 256
SWIGLU_LIMIT, SWIGLU_ALPHA = 7.0, 1.702
LN_EPS, RMS_EPS = 1e-5, 1e-6

COL_AQ, COL_AF, COL_AI, COL_AG = 0, 8, 16, 24
COL_BQ, COL_BK, COL_BV, COL_BZ = 32, 44, 56, 68
MAIN_COLS = 80 * LANES
COL_CQ, COL_CK, COL_CV, COL_CR = 0, 6, 12, 24
MISC_BETA, MISC_DECAY, MISC_CODE = 0, 12, 24


def _cparams(sem):
    return pltpu.CompilerParams(dimension_semantics=sem, vmem_limit_bytes=VMEM_LIMIT)


def _dot(a, b):
    return jnp.dot(a, b, preferred_element_type=F32)


def _dot_nt(a, b):
    return lax.dot_general(a, b, (((1,), (1,)), ((), ())), preferred_element_type=F32)


def _dot_tn(a, b):
    return lax.dot_general(a, b, (((0,), (0,)), ((), ())), preferred_element_type=F32)


def _split(x):
    hi = x.astype(BF16)
    lo = (x - hi.astype(F32)).astype(BF16)
    return hi, lo


def _dot_exact_lhs(sel2, x):
    hi, lo = _split(x)
    return _dot(sel2, jnp.concatenate([hi, lo], axis=0))


def _sigmoid(x):
    return 1.0 / (1.0 + jnp.exp(-x))


def _silu(x):
    return x * _sigmoid(x)


def _iota(shape, dim):
    return lax.broadcasted_iota(jnp.int32, shape, dim)


def _full_spec(shape):
    return pl.BlockSpec(shape, lambda *_: (0,) * len(shape))


def _mm_kernel(a_ref, w_ref, o_ref):
    o_ref[...] = _dot(a_ref[...], w_ref[...]).astype(o_ref.dtype)


def _matmul(a, w, layer, out_dtype, tm, tn):
    m, k = a.shape
    n = w.shape[2]
    return pl.pallas_call(
        _mm_kernel,
        grid=(n // tn, m // tm),
        in_specs=[pl.BlockSpec((tm, k), lambda j, i: (i, 0)),
                  pl.BlockSpec((None, k, tn), lambda j, i: (layer, 0, j))],
        out_specs=pl.BlockSpec((tm, tn), lambda j, i: (i, j)),
        out_shape=jax.ShapeDtypeStruct((m, n), out_dtype),
        compiler_params=_cparams(("arbitrary", "arbitrary")),
        name="matmul",
    )(a, w)


def _ln_rows(y, g, b):
    mu = jnp.mean(y, axis=-1, keepdims=True)
    d = y - mu
    var = jnp.mean(d * d, axis=-1, keepdims=True)
    return d * lax.rsqrt(var + LN_EPS) * g + b


def _ln_kernel(x_ref, g_ref, b_ref, o_ref):
    o_ref[...] = _ln_rows(x_ref[...], g_ref[...], b_ref[...]).astype(o_ref.dtype)


def _layer_norm(x, g, b, out_dtype, tm):
    m, d = x.shape
    return pl.pallas_call(
        _ln_kernel,
        grid=(m // tm,),
        in_specs=[pl.BlockSpec((tm, d), lambda i: (i, 0)),
                  pl.BlockSpec((1, d), lambda i: (0, 0)),
                  pl.BlockSpec((1, d), lambda i: (0, 0))],
        out_specs=pl.BlockSpec((tm, d), lambda i: (i, 0)),
        out_shape=jax.ShapeDtypeStruct((m, d), out_dtype),
        compiler_params=_cparams(("arbitrary",)),
        name="layer_norm",
    )(x, g.reshape(1, d), b.reshape(1, d))


def _proj_ln_kernel(a0_ref, a1_ref, a2_ref, w_ref, x_ref, g_ref, b_ref, of_ref, ob_ref, *, alpha, bounds):
    kk = pl.program_id(1)
    n0, n1, nk = bounds
    tm, d = of_ref.shape
    rows = [slice(r, r + ROW_SLICE) for r in range(0, tm, ROW_SLICE)]
    cols = [slice(c, c + COL_SLICE) for c in range(0, d, COL_SLICE)]

    @pl.when(kk == 0)
    def _():
        of_ref[...] = jnp.zeros_like(of_ref)

    def accumulate(a_ref):
        a = a_ref[...]
        for sl in cols:
            of_ref[:, sl] += _dot(a, w_ref[:, sl])

    pl.when(kk < n0)(lambda: accumulate(a0_ref))
    pl.when((kk >= n0) & (kk < n1))(lambda: accumulate(a1_ref))
    pl.when(kk >= n1)(lambda: accumulate(a2_ref))

    @pl.when(kk == nk - 1)
    def _():
        for sl in rows:
            y = _ln_rows(alpha * x_ref[sl, :] + of_ref[sl, :], g_ref[...], b_ref[...])
            of_ref[sl, :] = y
            ob_ref[sl, :] = y.astype(BF16)


def _proj_ln(a0, a1, a2, w, layer, x, g, b, alpha, tm, tk):
    m = a0.shape[0]
    d = w.shape[2]
    n0 = a0.shape[1] // tk
    n1 = n0 + a1.shape[1] // tk
    nk = n1 + a2.shape[1] // tk
    assert nk * tk == w.shape[1]
    return pl.pallas_call(
        functools.partial(_proj_ln_kernel, alpha=alpha, bounds=(n0, n1, nk)),
        grid=(m // tm, nk),
        in_specs=[pl.BlockSpec((tm, tk), lambda i, kk: (i, jnp.minimum(kk, n0 - 1))),
                  pl.BlockSpec((tm, tk), lambda i, kk: (i, jnp.clip(kk - n0, 0, n1 - n0 - 1))),
                  pl.BlockSpec((tm, tk), lambda i, kk: (i, jnp.clip(kk - n1, 0, nk - n1 - 1))),
                  pl.BlockSpec((None, tk, d), lambda i, kk: (layer, kk, 0)),
                  pl.BlockSpec((tm, d), lambda i, kk: (i, 0)),
                  pl.BlockSpec((1, d), lambda i, kk: (0, 0)),
                  pl.BlockSpec((1, d), lambda i, kk: (0, 0))],
        out_specs=[pl.BlockSpec((tm, d), lambda i, kk: (i, 0)),
                   pl.BlockSpec((tm, d), lambda i, kk: (i, 0))],
        out_shape=[jax.ShapeDtypeStruct((m, d), F32), jax.ShapeDtypeStruct((m, d), BF16)],
        compiler_params=_cparams(("arbitrary", "arbitrary")),
        name="proj_ln",
    )(a0, a1, a2, w, x, g.reshape(1, d), b.reshape(1, d))


_LEVELS = (32, 16, 8, 4, 2, 1)
_N_SEL = len(_LEVELS) + 2


def _range_selectors():
    i = np.arange(CHUNK)[:, None]
    t = np.arange(CHUNK)[None, :]
    blocks = [t <= i]
    for s in _LEVELS:
        m = (i & ~(2 * s - 1)) | (s - 1)
        later = (i & s) != 0
        blocks.append(np.where(later, (t > m) & (t <= i), (t > i) & (t <= m)))
    blocks.append(t > i)
    sel = np.concatenate(blocks, axis=0).astype(np.float32)
    return jnp.asarray(np.concatenate([sel, sel], axis=1), dtype=BF16)


def _level_masks():
    i = np.arange(CHUNK)[:, None]
    j = np.arange(CHUNK)[None, :]
    masks = [i == j]
    for s in _LEVELS:
        sh = s.bit_length()
        masks.append(((i >> sh) == (j >> sh)) & ((i & s) != 0) & ((j & s) == 0))
    return jnp.asarray(np.stack(masks).astype(np.float32))


def _diag_gated_block(q_ref, k_ref, v_ref, la_ref, e_ref, rsel_ref, mask_ref, state_t, nchunks, group, emit):
    dk = q_ref.shape[1]
    row = _iota((CHUNK, dk), 0)
    laters = [(row & s) != 0 for s in _LEVELS]
    rsel = rsel_ref[...]
    span = _N_SEL * CHUNK
    for g0 in range(0, nchunks, group):
        cs = list(range(g0, min(g0 + group, nchunks)))
        for c in cs:
            e_ref[c * span:(c + 1) * span, :] = _dot_exact_lhs(rsel, la_ref[c * CHUNK:(c + 1) * CHUNK, :])
        qs = [q_ref[c * CHUNK:(c + 1) * CHUNK, :] for c in cs]
        ks = [k_ref[c * CHUNK:(c + 1) * CHUNK, :] for c in cs]
        vbs = [v_ref[c * CHUNK:(c + 1) * CHUNK, :].astype(BF16) for c in cs]
        scores = [_dot_nt(q.astype(BF16), k.astype(BF16)) * mask_ref[0] for q, k in zip(qs, ks)]
        for n in range(len(_LEVELS)):
            zs = [(jnp.where(laters[n], q, k)
                   * jnp.exp(e_ref[c * span + (n + 1) * CHUNK:c * span + (n + 2) * CHUNK, :])).astype(BF16)
                  for c, q, k in zip(cs, qs, ks)]
            scores = [sc + _dot_nt(z, z) * mask_ref[n + 1] for sc, z in zip(scores, zs)]
        o_intra = [_dot(sc.astype(BF16), vb) for sc, vb in zip(scores, vbs)]
        q_dec = [(q * jnp.exp(e_ref[c * span:c * span + CHUNK, :])).astype(BF16) for c, q in zip(cs, qs)]
        k_dec = [(k * jnp.exp(e_ref[c * span + 7 * CHUNK:(c + 1) * span, :])).astype(BF16) for c, k in zip(cs, ks)]
        ds_t = [_dot_tn(vb, kd) for vb, kd in zip(vbs, k_dec)]
        for n, c in enumerate(cs):
            emit(c, o_intra[n] + _dot_nt(q_dec[n], state_t.astype(BF16)))
            decay = jnp.exp(e_ref[c * span + CHUNK - 1:c * span + CHUNK, :])
            state_t = state_t * decay + ds_t[n]
    return state_t


def _rms_gate(o, g, gate):
    y = o * lax.rsqrt(jnp.mean(o * o, axis=-1, keepdims=True) + RMS_EPS)
    return y * g * _silu(gate)


def _hgrn_kernel(q_ref, f_ref, i_ref, g_ref, lb_ref, ng_ref, rsel_ref, mask_ref, o_ref,
                 st_ref, qs_ref, ks_ref, la_ref, e_ref, *, nchunks, group):
    @pl.when(pl.program_id(2) == 0)
    def _():
        st_ref[...] = jnp.zeros_like(st_ref)

    lb = lb_ref[0]
    f = lb + (1.0 - lb) * _sigmoid(f_ref[...])
    qs_ref[...] = _silu(q_ref[...])
    ks_ref[...] = 1.0 - f
    la_ref[...] = jnp.log(f)
    ng = ng_ref[...]

    def emit(c, o):
        sl = slice(c * CHUNK, (c + 1) * CHUNK)
        o_ref[sl, :] = _rms_gate(o, ng, g_ref[sl, :]).astype(o_ref.dtype)

    st_ref[...] = _diag_gated_block(qs_ref, ks_ref, i_ref, la_ref, e_ref, rsel_ref, mask_ref, st_ref[...],
                                    nchunks, group, emit)


def _hgrn(proj, lb, norm_g, rsel, masks, bsz, seq, tblk, group):
    nt = seq // tblk

    def col(base):
        return pl.BlockSpec((tblk, LANES), lambda b, h, t: (b * nt + t, base + h))

    return pl.pallas_call(
        functools.partial(_hgrn_kernel, nchunks=tblk // CHUNK, group=group),
        grid=(bsz, A_HEADS, nt),
        in_specs=[col(COL_AQ), col(COL_AF), col(COL_AI), col(COL_AG),
                  pl.BlockSpec((1, 1, LANES), lambda b, h, t: (h, 0, 0)),
                  _full_spec((1, A_DV)), _full_spec(rsel.shape), _full_spec(masks.shape)],
        out_specs=pl.BlockSpec((tblk, A_DV), lambda b, h, t: (b * nt + t, h)),
        out_shape=jax.ShapeDtypeStruct((bsz * seq, A_HEADS * A_DV), BF16),
        scratch_shapes=[pltpu.VMEM((A_DV, HEAD_DK), F32), pltpu.VMEM((tblk, HEAD_DK), F32),
                        pltpu.VMEM((tblk, HEAD_DK), F32), pltpu.VMEM((tblk, HEAD_DK), F32),
                        pltpu.VMEM((tblk * _N_SEL, HEAD_DK), F32)],
        compiler_params=_cparams(("arbitrary", "arbitrary", "arbitrary")),
        name="hgrn2",
    )(proj, proj, proj, proj, lb, norm_g.reshape(1, A_DV), rsel, masks)


def _gla_kernel(q_ref, k_ref, v_ref, r_ref, misc_ref, wup_ref, bup_ref, ng_ref, rsel_ref, mask_ref, o_ref,
                st_ref, qs_ref, la_ref, e_ref, *, nchunks, group):
    @pl.when(pl.program_id(2) == 0)
    def _():
        st_ref[...] = jnp.zeros_like(st_ref)

    z = _dot(misc_ref[...].astype(BF16), wup_ref[0]) + bup_ref[0]
    la_ref[...] = (jnp.minimum(z, 0.0) - jnp.log(1.0 + jnp.exp(-jnp.abs(z)))) * (1.0 / C_TAU)
    qs_ref[...] = q_ref[...] * (HEAD_DK ** -0.5)
    ng = ng_ref[...]

    def emit(c, o):
        sl = slice(c * CHUNK, (c + 1) * CHUNK)
        o_ref[sl, :] = _rms_gate(o, ng, r_ref[sl, :]).astype(o_ref.dtype)

    st_ref[...] = _diag_gated_block(qs_ref, k_ref, v_ref, la_ref, e_ref, rsel_ref, mask_ref, st_ref[...],
                                    nchunks, group, emit)


def _gla(proj, misc, w_up_pad, b_up, norm_g, rsel, masks, bsz, seq, tblk, group):
    nt = seq // tblk

    def col(base):
        return pl.BlockSpec((tblk, LANES), lambda b, h, t: (b * nt + t, base + h))

    def col2(base):
        return pl.BlockSpec((tblk, C_DV), lambda b, h, t: (b * nt + t, base // 2 + h))

    return pl.pallas_call(
        functools.partial(_gla_kernel, nchunks=tblk // CHUNK, group=group),
        grid=(bsz, C_HEADS, nt),
        in_specs=[col(COL_CQ), col(COL_CK), col2(COL_CV), col2(COL_CR),
                  pl.BlockSpec((tblk, LANES), lambda b, h, t: (b * nt + t, 0)),
                  pl.BlockSpec((1, LANES, LANES), lambda b, h, t: (h, 0, 0)),
                  pl.BlockSpec((1, 1, LANES), lambda b, h, t: (h, 0, 0)),
                  _full_spec((1, C_DV)), _full_spec(rsel.shape), _full_spec(masks.shape)],
        out_specs=pl.BlockSpec((tblk, C_DV), lambda b, h, t: (b * nt + t, h)),
        out_shape=jax.ShapeDtypeStruct((bsz * seq, C_HEADS * C_DV), BF16),
        scratch_shapes=[pltpu.VMEM((C_DV, HEAD_DK), F32), pltpu.VMEM((tblk, HEAD_DK), F32),
                        pltpu.VMEM((tblk, HEAD_DK), F32), pltpu.VMEM((tblk * _N_SEL, HEAD_DK), F32)],
        compiler_params=_cparams(("arbitrary", "arbitrary", "arbitrary")),
        name="gla",
    )(proj, proj, proj, proj, misc, w_up_pad, b_up, norm_g.reshape(1, C_DV), rsel, masks)


def _times_p(lefts, p):
    n = p.shape[0]
    ph, pl_ = _split(p)
    halves = [_split(x) for x in lefts]
    by_hi = _dot(jnp.concatenate([h for pair in halves for h in pair], axis=0), ph)
    by_lo = _dot(jnp.concatenate([hi for hi, _ in halves], axis=0), pl_)
    return [by_hi[2 * i * n:(2 * i + 1) * n] + by_hi[(2 * i + 1) * n:(2 * i + 2) * n] + by_lo[i * n:(i + 1) * n]
            for i in range(len(lefts))]


def _unit_lower_inverses(neg_ms):
    shape = neg_ms[0].shape
    eye = jnp.where(_iota(shape, 0) == _iota(shape, 1), 1.0, 0.0)
    ts = [eye + m for m in neg_ms]
    ps = [_times_p([m], m)[0] for m in neg_ms]
    for _ in range(4):
        prods = [_times_p([t, p], p) for t, p in zip(ts, ps)]
        ts = [t + tp for t, (tp, _) in zip(ts, prods)]
        ps = [pp for _, pp in prods]
    return [t + _times_p([t], p)[0] for t, p in zip(ts, ps)]


def _gdn_block(buf_ref, states, nchunks, group, emits):
    nh = len(states)
    states = list(states)
    ri = _iota((CHUNK, CHUNK), 0)
    ci = _iota((CHUNK, CHUNK), 1)
    incl = ri >= ci
    strict = ri > ci
    tri = jnp.where(_iota((CHUNK, 2 * CHUNK), 0) >= (_iota((CHUNK, 2 * CHUNK), 1) & (CHUNK - 1)),
                    1.0, 0.0).astype(BF16)
    upper = jnp.where(strict, 1.0, 0.0)
    for g0 in range(0, nchunks, group):
        items = [(hd, c) for c in range(g0, min(g0 + group, nchunks)) for hd in range(nh)]
        sls = [(hd, slice(c * CHUNK, (c + 1) * CHUNK)) for hd, c in items]
        qs = [buf_ref[hd, 3, sl, :] for hd, sl in sls]
        ks = [buf_ref[hd, 4, sl, :] for hd, sl in sls]
        vs = [buf_ref[hd, 5, sl, :] for hd, sl in sls]
        betas = [buf_ref[hd, 6, sl, :] for hd, sl in sls]
        gs = [buf_ref[hd, 7, sl, :] for hd, sl in sls]
        gcs = [_dot_exact_lhs(tri, g) for g in gs]
        dmats = [_dot_exact_lhs(tri, g[:, :CHUNK] * upper) for g in gs]
        gammas = [jnp.where(incl, jnp.exp(jnp.where(incl, dm, 0.0)), 0.0) for dm in dmats]
        qbs = [q.astype(BF16) for q in qs]
        kbs = [k.astype(BF16) for k in ks]
        kks = [_dot_nt(kb, kb) for kb in kbs]
        ts = _unit_lower_inverses([-jnp.where(strict, beta[:, :CHUNK] * kk * gamma, 0.0)
                                   for beta, kk, gamma in zip(betas, kks, gammas)])
        egcs = [jnp.exp(gc) for gc in gcs]
        rhss = [jnp.concatenate([k * (beta * egc), v * beta], axis=1).astype(BF16)
                for k, v, beta, egc in zip(ks, vs, betas, egcs)]
        wus = []
        for t, rhs in zip(ts, rhss):
            both = _dot(jnp.concatenate(_split(t), axis=0), rhs)
            wus.append((both[:CHUNK] + both[CHUNK:]).astype(BF16))
        attns = [jnp.where(incl, _dot_nt(qb, kb) * gamma, 0.0).astype(BF16)
                 for qb, kb, gamma in zip(qbs, kbs, gammas)]
        gls = [gc[CHUNK - 1:CHUNK, :] for gc in gcs]
        k_decs = [(k * jnp.exp(gl - gc)).astype(BF16) for k, gl, gc in zip(ks, gls, gcs)]
        kwus = [_dot_tn(kd, wu) for kd, wu in zip(k_decs, wus)]
        awus = [_dot(a, wu) for a, wu in zip(attns, wus)]
        q_effs = [(q * egc - awu[:, :HEAD_DK]).astype(BF16) for q, egc, awu in zip(qs, egcs, awus)]
        for n, (hd, c) in enumerate(items):
            st_b = states[hd].astype(BF16)
            emits[hd](c, _dot(q_effs[n], st_b) + awus[n][:, HEAD_DK:])
            states[hd] = (states[hd] * jnp.exp(gls[n]) - _dot(kwus[n][:, :HEAD_DK].astype(BF16), st_b)
                          + kwus[n][:, HEAD_DK:])
    return states


def _l2n(x):
    return x * lax.rsqrt(jnp.sum(x * x, axis=-1, keepdims=True) + RMS_EPS)


def _gdn_kernel(q_ref, k_ref, v_ref, z_ref, misc_ref, cw_ref, al_ref, dt_ref, ng_ref, o_ref,
                st_ref, buf_ref, *, nchunks, tblk, group):
    first_head = pl.program_id(1) * GDN_HEADS_PER_STEP

    @pl.when(pl.program_id(2) == 0)
    def _():
        st_ref[...] = jnp.zeros_like(st_ref)
        buf_ref[...] = jnp.zeros_like(buf_ref)

    mh, ml = _split(misc_ref[...])
    lane_r = _iota((LANES, LANES), 0)
    ng = ng_ref[...]
    emits = []
    for hd in range(GDN_HEADS_PER_STEP):
        lanes = slice(hd * LANES, (hd + 1) * LANES)
        conv = []
        for n, ref in enumerate((q_ref, k_ref, v_ref)):
            buf_ref[hd, n, 8:, :] = ref[:, lanes]
            acc = jnp.zeros((tblk, LANES), F32)
            for j in range(B_CONV):
                lo = 8 - (B_CONV - 1) + j
                acc = acc + cw_ref[hd, n, j:j + 1, :] * buf_ref[hd, n, lo:lo + tblk, :]
            buf_ref[hd, n, 0:8, :] = buf_ref[hd, n, tblk:tblk + 8, :]
            conv.append(_silu(acc))
        buf_ref[hd, 3, 0:tblk, :] = _l2n(conv[0]) * (HEAD_DK ** -0.5)
        buf_ref[hd, 4, 0:tblk, :] = _l2n(conv[1])
        buf_ref[hd, 5, 0:tblk, :] = conv[2]

        sel_beta = jnp.where(lane_r == MISC_BETA + first_head + hd, 1.0, 0.0).astype(BF16)
        sel_dec = jnp.where(lane_r == MISC_DECAY + first_head + hd, 1.0, 0.0).astype(BF16)
        beta_logit = _dot(mh, sel_beta) + _dot(ml, sel_beta)
        dec_in = _dot(mh, sel_dec) + _dot(ml, sel_dec) + dt_ref[hd]
        softplus = jnp.maximum(dec_in, 0.0) + jnp.log(1.0 + jnp.exp(-jnp.abs(dec_in)))
        buf_ref[hd, 6, 0:tblk, :] = _sigmoid(beta_logit)
        buf_ref[hd, 7, 0:tblk, :] = -jnp.exp(al_ref[hd]) * softplus

        def emit(c, o, lanes=lanes):
            sl = slice(c * CHUNK, (c + 1) * CHUNK)
            o_ref[sl, lanes] = _rms_gate(o, ng, z_ref[sl, lanes]).astype(o_ref.dtype)

        emits.append(emit)

    states = _gdn_block(buf_ref, [st_ref[hd] for hd in range(GDN_HEADS_PER_STEP)], nchunks, group, emits)
    for hd in range(GDN_HEADS_PER_STEP):
        st_ref[hd] = states[hd]


def _gdn(proj, misc, conv_w, a_log, dt_bias, norm_g, bsz, seq, tblk, group):
    nt = seq // tblk
    hps = GDN_HEADS_PER_STEP
    width = hps * LANES

    def col(base):
        return pl.BlockSpec((tblk, width), lambda b, p, t: (b * nt + t, base // hps + p))

    def per_group(shape):
        return pl.BlockSpec((None, hps) + shape, lambda b, p, t: (p, 0) + (0,) * len(shape))

    return pl.pallas_call(
        functools.partial(_gdn_kernel, nchunks=tblk // CHUNK, tblk=tblk, group=group),
        grid=(bsz, B_HEADS // hps, nt),
        in_specs=[col(COL_BQ), col(COL_BK), col(COL_BV), col(COL_BZ),
                  pl.BlockSpec((tblk, LANES), lambda b, p, t: (b * nt + t, 0)),
                  per_group((3, B_CONV, LANES)), per_group((1, LANES)), per_group((1, LANES)),
                  _full_spec((1, B_DV))],
        out_specs=pl.BlockSpec((tblk, width), lambda b, p, t: (b * nt + t, p)),
        out_shape=jax.ShapeDtypeStruct((bsz * seq, B_HEADS * B_DV), BF16),
        scratch_shapes=[pltpu.VMEM((hps, HEAD_DK, B_DV), F32), pltpu.VMEM((hps, 8, tblk + 8, LANES), F32)],
        compiler_params=_cparams(("arbitrary", "arbitrary", "arbitrary")),
        name="gdn",
    )(proj, proj, proj, proj, misc,
      conv_w.reshape((B_HEADS // hps, hps) + conv_w.shape[1:]),
      a_log.reshape(B_HEADS // hps, hps, 1, LANES), dt_bias.reshape(B_HEADS // hps, hps, 1, LANES),
      norm_g.reshape(1, B_DV))


def _xattn_kernel(xf_ref, wq_ref, kv_ref, wo_ref, g_ref, b_ref, of_ref, ob_ref, *, alpha):
    q = _dot(xf_ref[...].astype(BF16), wq_ref[...])
    width = X_HEADS * X_HEAD_DIM
    outs = []
    for hd in range(X_HEADS):
        lo = hd * X_HEAD_DIM
        qh = q[:, lo:lo + X_HEAD_DIM].astype(BF16)
        kh = kv_ref[:, lo:lo + X_HEAD_DIM]
        vh = kv_ref[:, width + lo:width + lo + X_HEAD_DIM]
        s = _dot_nt(qh, kh) * (X_HEAD_DIM ** -0.5)
        s = s - jnp.max(s, axis=-1, keepdims=True)
        p = jnp.exp(s)
        p = p / jnp.sum(p, axis=-1, keepdims=True)
        outs.append(_dot(p.astype(BF16), vh))
    o = jnp.concatenate(outs, axis=1).astype(BF16)
    tq, d = of_ref.shape
    for c in range(0, d, COL_SLICE):
        of_ref[:, c:c + COL_SLICE] = _dot(o, wo_ref[:, c:c + COL_SLICE])
    for r in range(0, tq, ROW_SLICE):
        sl = slice(r, r + ROW_SLICE)
        y = _ln_rows(alpha * xf_ref[sl, :] + of_ref[sl, :], g_ref[...], b_ref[...])
        of_ref[sl, :] = y
        ob_ref[sl, :] = y.astype(BF16)


def _xattn(xf, wq, kv, wo, layer, g, b, alpha, bsz, seq, tq):
    d = xf.shape[1]
    nt = seq // tq
    width = X_HEADS * X_HEAD_DIM
    n_mem = kv.shape[1]
    return pl.pallas_call(
        functools.partial(_xattn_kernel, alpha=alpha),
        grid=(bsz, nt),
        in_specs=[pl.BlockSpec((tq, d), lambda bb, t: (bb * nt + t, 0)),
                  pl.BlockSpec((None, d, width), lambda bb, t: (layer, 0, 0), pipeline_mode=pl.Buffered(1)),
                  pl.BlockSpec((None, n_mem, 2 * width), lambda bb, t: (bb, 0, 0)),
                  pl.BlockSpec((None, width, d), lambda bb, t: (layer, 0, 0), pipeline_mode=pl.Buffered(1)),
                  pl.BlockSpec((1, d), lambda bb, t: (0, 0)),
                  pl.BlockSpec((1, d), lambda bb, t: (0, 0))],
        out_specs=[pl.BlockSpec((tq, d), lambda bb, t: (bb * nt + t, 0)),
                   pl.BlockSpec((tq, d), lambda bb, t: (bb * nt + t, 0))],
        out_shape=[jax.ShapeDtypeStruct((bsz * seq, d), F32), jax.ShapeDtypeStruct((bsz * seq, d), BF16)],
        compiler_params=_cparams(("arbitrary", "arbitrary")),
        name="xattn",
    )(xf, wq, kv, wo, g.reshape(1, d), b.reshape(1, d))


def _router_kernel(xb_ref, w_ref, b_ref, o_ref):
    logits = _dot(xb_ref[...], w_ref[...]) + b_ref[...]
    lane = _iota(logits.shape, 1).astype(F32)
    neg = jnp.float32(-3.0e38)
    picked = []
    vals = []
    work = logits
    for _ in range(TOP_K):
        mx = jnp.max(work, axis=-1, keepdims=True)
        idx = jnp.min(jnp.where(work == mx, lane, float(LANES)), axis=-1, keepdims=True)
        sel = lane == idx
        picked.append(sel)
        vals.append(mx)
        work = jnp.where(sel, neg, work)
    es = [jnp.exp(vv - vals[0]) for vv in vals]
    inv = 1.0 / (es[0] + es[1] + es[2] + es[3])
    comb = jnp.zeros_like(logits)
    for sel, e in zip(picked, es):
        comb = comb + jnp.where(sel, e * inv, 0.0)
    o_ref[...] = comb


def _router(xb, w_pad, b_pad, tm):
    m, d = xb.shape
    return pl.pallas_call(
        _router_kernel,
        grid=(m // tm,),
        in_specs=[pl.BlockSpec((tm, d), lambda i: (i, 0)),
                  pl.BlockSpec((d, LANES), lambda i: (0, 0)),
                  pl.BlockSpec((1, LANES), lambda i: (0, 0))],
        out_specs=pl.BlockSpec((tm, LANES), lambda i: (i, 0)),
        out_shape=jax.ShapeDtypeStruct((m, LANES), F32),
        compiler_params=_cparams(("arbitrary",)),
        name="router",
    )(xb, w_pad, b_pad)


def _moe_up_kernel(xb_ref, comb_ref, wg_ref, bg_ref, wu_ref, bu_ref, h_ref):
    e = pl.program_id(1)
    sel = jnp.where(_iota((LANES, EXPERT_FF), 0) == e, 1.0, 0.0).astype(BF16)
    xb = xb_ref[...]
    gate = jnp.minimum(_dot(xb, wg_ref[...].astype(BF16)) + bg_ref[...], SWIGLU_LIMIT)
    lin = jnp.clip(_dot(xb, wu_ref[...].astype(BF16)) + bu_ref[...], -SWIGLU_LIMIT, SWIGLU_LIMIT)
    hid = (lin + 1.0) * gate * _sigmoid(SWIGLU_ALPHA * gate)
    ch, cl = _split(comb_ref[...])
    weight = _dot(ch, sel) + _dot(cl, sel)
    h_ref[...] = (hid * weight).astype(BF16)


def _moe_up(xb, comb, wg, bg, wu, bu, layer, ne, tm):
    m, d = xb.shape
    ff = wg.shape[2]
    base = layer * ne
    return pl.pallas_call(
        _moe_up_kernel,
        grid=(m // tm, ne),
        in_specs=[pl.BlockSpec((tm, d), lambda i, e: (i, 0)),
                  pl.BlockSpec((tm, LANES), lambda i, e: (i, 0)),
                  pl.BlockSpec((None, d, ff), lambda i, e: (base + e, 0, 0)),
                  pl.BlockSpec((None, 1, ff), lambda i, e: (base + e, 0, 0)),
                  pl.BlockSpec((None, d, ff), lambda i, e: (base + e, 0, 0)),
                  pl.BlockSpec((None, 1, ff), lambda i, e: (base + e, 0, 0))],
        out_specs=pl.BlockSpec((tm, ff), lambda i, e: (i, e)),
        out_shape=jax.ShapeDtypeStruct((m, ne * ff), BF16),
        compiler_params=_cparams(("arbitrary", "arbitrary")),
        name="moe_up",
    )(xb, comb, wg, bg, wu, bu)


def _moe_down_kernel(h_ref, wd_ref, comb_ref, bd_ref, xf_ref, g_ref, b_ref, of_ref, ob_ref, *, alpha, nk):
    kk = pl.program_id(1)
    tm, d = of_ref.shape

    @pl.when(kk == 0)
    def _():
        of_ref[...] = jnp.zeros_like(of_ref)

    h = h_ref[...]
    for c in range(0, d, COL_SLICE):
        of_ref[:, c:c + COL_SLICE] += _dot(h, wd_ref[:, c:c + COL_SLICE])

    @pl.when(kk == nk - 1)
    def _():
        for r in range(0, tm, ROW_SLICE):
            sl = slice(r, r + ROW_SLICE)
            y = of_ref[sl, :] + _dot(comb_ref[sl, :].astype(BF16), bd_ref[...])
            y = _ln_rows(alpha * xf_ref[sl, :] + y, g_ref[...], b_ref[...])
            of_ref[sl, :] = y
            ob_ref[sl, :] = y.astype(BF16)


def _moe_down(h, wd, layer, comb, bd_pad, xf, g, b, alpha, tm, tk):
    m, d = xf.shape
    nk = h.shape[1] // tk
    once = pl.Buffered(1)
    return pl.pallas_call(
        functools.partial(_moe_down_kernel, alpha=alpha, nk=nk),
        grid=(m // tm, nk),
        in_specs=[pl.BlockSpec((tm, tk), lambda i, kk: (i, kk)),
                  pl.BlockSpec((None, tk, d), lambda i, kk: (layer, kk, 0)),
                  pl.BlockSpec((tm, LANES), lambda i, kk: (i, 0), pipeline_mode=once),
                  pl.BlockSpec((LANES, d), lambda i, kk: (0, 0), pipeline_mode=once),
                  pl.BlockSpec((tm, d), lambda i, kk: (i, 0), pipeline_mode=once),
                  pl.BlockSpec((1, d), lambda i, kk: (0, 0)),
                  pl.BlockSpec((1, d), lambda i, kk: (0, 0))],
        out_specs=[pl.BlockSpec((tm, d), lambda i, kk: (i, 0)),
                   pl.BlockSpec((tm, d), lambda i, kk: (i, 0))],
        out_shape=[jax.ShapeDtypeStruct((m, d), F32), jax.ShapeDtypeStruct((m, d), BF16)],
        compiler_params=_cparams(("arbitrary", "arbitrary")),
        name="moe_down",
    )(h, wd, comb, bd_pad, xf, g.reshape(1, d), b.reshape(1, d))


def _pick(n, prefs):
    for p in prefs:
        if n % p == 0:
            return p
    raise ValueError(f"no tile in {prefs} divides {n}")


def kernel(x, mem, w_in, hgrn_lb_raw, hgrn_norm_g, gdn_conv_w, gdn_a_log, gdn_dt_bias, gdn_norm_g, gla_w_up, gla_b_up, gla_norm_g, w_out, ln_mix_g, ln_mix_b, mem_ln_g, mem_ln_b, xattn_wq, xattn_wkv, xattn_wo, ln_xattn_g, ln_xattn_b, w_router, b_router, w_gate, b_gate, w_up, b_up, w_down, b_down, ln_ffn_g, ln_ffn_b):
    bsz, seq, d = x.shape
    depth = w_in.shape[0]
    n_mem = mem.shape[1]
    ntok = bsz * seq
    alpha = float((2 * depth) ** 0.25)
    assert seq % CHUNK == 0 and d % LANES == 0

    tblk = _pick(seq, (1024, 512, 256, 128, 64))
    group = min(8, tblk // CHUNK)
    tblk_gdn = _pick(seq, (512, 256, 128, 64))
    tm_big = _pick(ntok, (512, 256, 128, 64))
    tm_mid = _pick(ntok, (256, 128, 64))
    tm_up = _pick(ntok, (1024, 512, 256, 128, 64))

    a_w, b_w, c_w = A_HEADS * HEAD_DK, B_HEADS * HEAD_DK, C_HEADS * HEAD_DK
    o_bb = 4 * a_w + (2 * b_w + B_HEADS * B_DV) + B_HEADS * B_DV
    o_cq = o_bb + 2 * B_HEADS
    o_clr = o_cq + 2 * c_w + 2 * C_HEADS * C_DV
    n_cols = o_clr + C_RANK
    assert w_in.shape[2] == n_cols
    assert o_bb == MAIN_COLS
    w_main = w_in[:, :, :o_bb].astype(BF16)
    w_c = w_in[:, :, o_cq:o_clr].astype(BF16)
    w_misc = jnp.concatenate([w_in[:, :, o_bb:o_cq], w_in[:, :, o_clr:],
                              jnp.zeros((depth, d, LANES - 2 * B_HEADS - C_RANK), w_in.dtype)],
                             axis=2).astype(BF16)
    w_out_b = w_out.astype(BF16)
    wq_b, wkv_b, wo_b = xattn_wq.astype(BF16), xattn_wkv.astype(BF16), xattn_wo.astype(BF16)
    wr_p = jnp.concatenate([w_router, jnp.zeros((depth, d, LANES - N_EXPERTS), F32)], axis=2).astype(BF16)
    br_p = jnp.concatenate([b_router, jnp.full((depth, LANES - N_EXPERTS), -1.0e30, F32)], axis=1)
    ne, ff = w_gate.shape[1], w_gate.shape[3]
    wg_r = w_gate.reshape(depth * ne, d, ff)
    wu_r = w_up.reshape(depth * ne, d, ff)
    wd_b = w_down.astype(BF16).reshape(depth, ne * ff, d)
    bg_r = b_gate.reshape(depth * ne, 1, ff)
    bu_r = b_up.reshape(depth * ne, 1, ff)
    bd_p = jnp.concatenate([b_down, jnp.zeros((depth, LANES - N_EXPERTS, d), F32)], axis=1).astype(BF16)
    gla_wup_p = jnp.zeros((depth, LANES, c_w), F32).at[:, MISC_CODE:MISC_CODE + C_RANK, :].set(gla_w_up)
    gla_wup_p = gla_wup_p.reshape(depth, LANES, C_HEADS, HEAD_DK).transpose(0, 2, 1, 3).astype(BF16)
    gla_bup_r = gla_b_up.reshape(depth, C_HEADS, 1, HEAD_DK)
    conv_r = gdn_conv_w.reshape(depth, B_CONV, 3, B_HEADS, HEAD_DK).transpose(0, 3, 2, 1, 4)
    a_log_r = jnp.broadcast_to(gdn_a_log[:, :, None, None], (depth, B_HEADS, 1, LANES))
    dt_r = jnp.broadcast_to(gdn_dt_bias[:, :, None, None], (depth, B_HEADS, 1, LANES))
    lb_all = jnp.cumsum(jax.nn.softmax(hgrn_lb_raw.astype(F32), axis=0), axis=0)
    lb_r = (lb_all - lb_all[0:1]).reshape(depth, A_HEADS, 1, HEAD_DK)
    rsel = _range_selectors()
    masks = _level_masks()

    tm_mem = _pick(bsz * n_mem, (256, 128, 64, 8))
    mem_n = _layer_norm(mem.reshape(bsz * n_mem, d), mem_ln_g, mem_ln_b, BF16, tm_mem)

    xf = x.reshape(ntok, d)
    xb = xf.astype(BF16)
    for l in range(depth):
        proj = _matmul(xb, w_main, l, F32, tm_big, 1280)
        proj_c = _matmul(xb, w_c, l, F32, tm_big, 1536)
        misc = _matmul(xb, w_misc, l, F32, tm_big, LANES)
        oa = _hgrn(proj, lb_r[l], hgrn_norm_g[l], rsel, masks, bsz, seq, tblk, group)
        ob = _gdn(proj, misc, conv_r[l], a_log_r[l], dt_r[l], gdn_norm_g[l], bsz, seq, tblk_gdn, group)
        oc = _gla(proj_c, misc, gla_wup_p[l], gla_bup_r[l], gla_norm_g[l], rsel, masks, bsz, seq, tblk, group)
        xf, xb = _proj_ln(oa, ob, oc, w_out_b, l, xf, ln_mix_g[l], ln_mix_b[l], alpha, tm_big, 512)

        kv = _matmul(mem_n, wkv_b, l, BF16, tm_mem, 2 * X_HEADS * X_HEAD_DIM)
        xf, xb = _xattn(xf, wq_b, kv.reshape(bsz, n_mem, -1), wo_b, l, ln_xattn_g[l], ln_xattn_b[l],
                        alpha, bsz, seq, _pick(seq, (512, 256, 128, 64)))

        comb = _router(xb, wr_p[l], br_p[l, None, :], tm_big)
        hid = _moe_up(xb, comb, wg_r, bg_r, wu_r, bu_r, l, ne, tm_up)
        xf, xb = _moe_down(hid, wd_b, l, comb, bd_p[l], xf, ln_ffn_g[l], ln_ffn_b[l], alpha, tm_big, 1024)
    return xf.reshape(bsz, seq, d)
```

```python
import functools

import numpy as np
import jax
import jax.numpy as jnp
from jax import lax
from jax.experimental import pallas as pl
from jax.experimental.pallas import tpu as pltpu

F32 = jnp.float32
BF16 = jnp.bfloat16

LANES = 128
CHUNK = 64
VMEM_LIMIT = 62 * 1024 * 1024
ROW_SLICE = 256
COL_SLICE = 1024
ATTN_ROWS = 128

A_HEADS, B_HEADS, C_HEADS = 8, 12, 6
HEAD_DK = 128
A_DV, B_DV, C_DV = 128, 128, 256
B_CONV = 4
GDN_HEADS_PER_STEP = 4
DIAG_HEADS_PER_STEP = 2
C_RANK = 16
C_TAU = 16.0
X_HEADS, X_HEAD_DIM = 4, 128
N_EXPERTS, TOP_K, EXPERT_FF = 32, 4, 256
SWIGLU_LIMIT, SWIGLU_ALPHA = 7.0, 1.702
LN_EPS, RMS_EPS = 1e-5, 1e-6

COL_AQ, COL_AF, COL_AI, COL_AG = 0, 8, 16, 24
COL_BQ, COL_BK, COL_BV, COL_BZ = 32, 44, 56, 68
MAIN_COLS = 80 * LANES
COL_CQ, COL_CK, COL_CV, COL_CR = 0, 6, 12, 24
MISC_BETA, MISC_DECAY, MISC_CODE = 0, 12, 24


def _cparams(sem):
    return pltpu.CompilerParams(dimension_semantics=sem, vmem_limit_bytes=VMEM_LIMIT)


def _dot(a, b):
    return jnp.dot(a, b, preferred_element_type=F32)


def _dot_nt(a, b):
    return lax.dot_general(a, b, (((1,), (1,)), ((), ())), preferred_element_type=F32)


def _dot_tn(a, b):
    return lax.dot_general(a, b, (((0,), (0,)), ((), ())), preferred_element_type=F32)


def _split(x):
    hi = x.astype(BF16)
    lo = (x - hi.astype(F32)).astype(BF16)
    return hi, lo


def _dot_exact_lhs(sel2, x):
    hi, lo = _split(x)
    return _dot(sel2, jnp.concatenate([hi, lo], axis=0))


def _sigmoid(x):
    return 1.0 / (1.0 + jnp.exp(-x))


def _silu(x):
    return x * _sigmoid(x)


def _iota(shape, dim):
    return lax.broadcasted_iota(jnp.int32, shape, dim)


def _full_spec(shape):
    return pl.BlockSpec(shape, lambda *_: (0,) * len(shape))


def _mm_kernel(a_ref, w_ref, o_ref):
    o_ref[...] = _dot(a_ref[...], w_ref[...]).astype(o_ref.dtype)


def _matmul(a, w, layer, out_dtype, tm, tn):
    m, k = a.shape
    n = w.shape[2]
    return pl.pallas_call(
        _mm_kernel,
        grid=(n // tn, m // tm),
        in_specs=[pl.BlockSpec((tm, k), lambda j, i: (i, 0)),
                  pl.BlockSpec((None, k, tn), lambda j, i: (layer, 0, j))],
        out_specs=pl.BlockSpec((tm, tn), lambda j, i: (i, j)),
        out_shape=jax.ShapeDtypeStruct((m, n), out_dtype),
        compiler_params=_cparams(("arbitrary", "arbitrary")),
        name="matmul",
    )(a, w)


def _ln_rows(y, g, b):
    mu = jnp.mean(y, axis=-1, keepdims=True)
    d = y - mu
    var = jnp.mean(d * d, axis=-1, keepdims=True)
    return d * lax.rsqrt(var + LN_EPS) * g + b


def _ln_kernel(x_ref, g_ref, b_ref, o_ref):
    o_ref[...] = _ln_rows(x_ref[...], g_ref[...], b_ref[...]).astype(o_ref.dtype)


def _layer_norm(x, g, b, out_dtype, tm):
    m, d = x.shape
    return pl.pallas_call(
        _ln_kernel,
        grid=(m // tm,),
        in_specs=[pl.BlockSpec((tm, d), lambda i: (i, 0)),
                  pl.BlockSpec((1, d), lambda i: (0, 0)),
                  pl.BlockSpec((1, d), lambda i: (0, 0))],
        out_specs=pl.BlockSpec((tm, d), lambda i: (i, 0)),
        out_shape=jax.ShapeDtypeStruct((m, d), out_dtype),
        compiler_params=_cparams(("arbitrary",)),
        name="layer_norm",
    )(x, g.reshape(1, d), b.reshape(1, d))


def _proj_ln_kernel(a0_ref, a1_ref, a2_ref, w_ref, x_ref, g_ref, b_ref, of_ref, ob_ref, *, alpha, bounds):
    kk = pl.program_id(1)
    n0, n1, nk = bounds
    tm, d = of_ref.shape
    rows = [slice(r, r + ROW_SLICE) for r in range(0, tm, ROW_SLICE)]
    cols = [slice(c, c + COL_SLICE) for c in range(0, d, COL_SLICE)]

    @pl.when(kk == 0)
    def _():
        of_ref[...] = jnp.zeros_like(of_ref)

    def accumulate(a_ref):
        a = a_ref[...]
        for sl in cols:
            of_ref[:, sl] += _dot(a, w_ref[:, sl])

    pl.when(kk < n0)(lambda: accumulate(a0_ref))
    pl.when((kk >= n0) & (kk < n1))(lambda: accumulate(a1_ref))
    pl.when(kk >= n1)(lambda: accumulate(a2_ref))

    @pl.when(kk == nk - 1)
    def _():
        for sl in rows:
            y = _ln_rows(alpha * x_ref[sl, :] + of_ref[sl, :], g_ref[...], b_ref[...])
            of_ref[sl, :] = y
            ob_ref[sl, :] = y.astype(BF16)


def _proj_ln(a0, a1, a2, w, layer, x, g, b, alpha, tm, tk):
    m = a0.shape[0]
    d = w.shape[2]
    n0 = a0.shape[1] // tk
    n1 = n0 + a1.shape[1] // tk
    nk = n1 + a2.shape[1] // tk
    assert nk * tk == w.shape[1]
    return pl.pallas_call(
        functools.partial(_proj_ln_kernel, alpha=alpha, bounds=(n0, n1, nk)),
        grid=(m // tm, nk),
        in_specs=[pl.BlockSpec((tm, tk), lambda i, kk: (i, jnp.minimum(kk, n0 - 1))),
                  pl.BlockSpec((tm, tk), lambda i, kk: (i, jnp.clip(kk - n0, 0, n1 - n0 - 1))),
                  pl.BlockSpec((tm, tk), lambda i, kk: (i, jnp.clip(kk - n1, 0, nk - n1 - 1))),
                  pl.BlockSpec((None, tk, d), lambda i, kk: (layer, kk, 0)),
                  pl.BlockSpec((tm, d), lambda i, kk: (i, 0)),
                  pl.BlockSpec((1, d), lambda i, kk: (0, 0)),
                  pl.BlockSpec((1, d), lambda i, kk: (0, 0))],
        out_specs=[pl.BlockSpec((tm, d), lambda i, kk: (i, 0)),
                   pl.BlockSpec((tm, d), lambda i, kk: (i, 0))],
        out_shape=[jax.ShapeDtypeStruct((m, d), F32), jax.ShapeDtypeStruct((m, d), BF16)],
        compiler_params=_cparams(("arbitrary", "arbitrary")),
        name="proj_ln",
    )(a0, a1, a2, w, x, g.reshape(1, d), b.reshape(1, d))


_LEVELS = (32, 16, 8, 4, 2, 1)
_N_SEL = len(_LEVELS) + 2


def _range_selectors():
    i = np.arange(CHUNK)[:, None]
    t = np.arange(CHUNK)[None, :]
    blocks = [t <= i]
    for s in _LEVELS:
        m = (i & ~(2 * s - 1)) | (s - 1)
        later = (i & s) != 0
        blocks.append(np.where(later, (t > m) & (t <= i), (t > i) & (t <= m)))
    blocks.append(t > i)
    sel = np.concatenate(blocks, axis=0).astype(np.float32)
    return jnp.asarray(np.concatenate([sel, sel], axis=1), dtype=BF16)


def _level_masks():
    i = np.arange(CHUNK)[:, None]
    j = np.arange(CHUNK)[None, :]
    masks = [i == j]
    for s in _LEVELS:
        sh = s.bit_length()
        masks.append(((i >> sh) == (j >> sh)) & ((i & s) != 0) & ((j & s) == 0))
    return jnp.asarray(np.stack(masks).astype(np.float32))


def _diag_gated_block(get_q, get_k, get_v, get_la, e_ref, rsel_ref, mask_ref, states_t, nchunks, group, emits):
    nh = len(states_t)
    states_t = list(states_t)
    dk = e_ref.shape[2]
    row = _iota((CHUNK, dk), 0)
    laters = [(row & s) != 0 for s in _LEVELS]
    rsel = rsel_ref[...]
    span = _N_SEL * CHUNK
    for g0 in range(0, nchunks, group):
        items = [(hd, c) for c in range(g0, min(g0 + group, nchunks)) for hd in range(nh)]
        rows = [slice(c * CHUNK, (c + 1) * CHUNK) for _, c in items]
        for (hd, c), sl in zip(items, rows):
            e_ref[hd, c * span:(c + 1) * span, :] = _dot_exact_lhs(rsel, get_la(hd, sl))

        def e_rows(hd, c, first, last):
            return e_ref[hd, c * span + first:c * span + last, :]

        qs = [get_q(hd, sl) for (hd, _), sl in zip(items, rows)]
        ks = [get_k(hd, sl) for (hd, _), sl in zip(items, rows)]
        vbs = [get_v(hd, sl).astype(BF16) for (hd, _), sl in zip(items, rows)]
        scores = [_dot_nt(q.astype(BF16), k.astype(BF16)) * mask_ref[0] for q, k in zip(qs, ks)]
        for n in range(len(_LEVELS)):
            zs = [(jnp.where(laters[n], q, k)
                   * jnp.exp(e_rows(hd, c, (n + 1) * CHUNK, (n + 2) * CHUNK))).astype(BF16)
                  for (hd, c), q, k in zip(items, qs, ks)]
            scores = [sc + _dot_nt(z, z) * mask_ref[n + 1] for sc, z in zip(scores, zs)]
        o_intra = [_dot(sc.astype(BF16), vb) for sc, vb in zip(scores, vbs)]
        q_dec = [(q * jnp.exp(e_rows(hd, c, 0, CHUNK))).astype(BF16) for (hd, c), q in zip(items, qs)]
        k_dec = [(k * jnp.exp(e_rows(hd, c, 7 * CHUNK, span))).astype(BF16) for (hd, c), k in zip(items, ks)]
        ds_t = [_dot_tn(vb, kd) for vb, kd in zip(vbs, k_dec)]
        for n, (hd, c) in enumerate(items):
            emits[hd](c, o_intra[n] + _dot_nt(q_dec[n], states_t[hd].astype(BF16)))
            decay = jnp.exp(e_rows(hd, c, CHUNK - 1, CHUNK))
            states_t[hd] = states_t[hd] * decay + ds_t[n]
    return states_t


def _rms_gate(o, g, gate):
    y = o * lax.rsqrt(jnp.mean(o * o, axis=-1, keepdims=True) + RMS_EPS)
    return y * g * _silu(gate)


def _hgrn_kernel(q_ref, f_ref, i_ref, g_ref, lb_ref, ng_ref, rsel_ref, mask_ref, o_ref,
                 st_ref, qs_ref, ks_ref, la_ref, e_ref, *, nchunks, group):
    @pl.when(pl.program_id(2) == 0)
    def _():
        st_ref[...] = jnp.zeros_like(st_ref)

    ng = ng_ref[...]
    lanes = [slice(hd * LANES, (hd + 1) * LANES) for hd in range(DIAG_HEADS_PER_STEP)]
    emits = []
    for hd, ln in enumerate(lanes):
        lb = lb_ref[hd]
        f = lb + (1.0 - lb) * _sigmoid(f_ref[:, ln])
        qs_ref[hd] = _silu(q_ref[:, ln])
        ks_ref[hd] = 1.0 - f
        la_ref[hd] = jnp.log(f)

        def emit(c, o, ln=ln):
            sl = slice(c * CHUNK, (c + 1) * CHUNK)
            o_ref[sl, ln] = _rms_gate(o, ng, g_ref[sl, ln]).astype(o_ref.dtype)

        emits.append(emit)

    states = _diag_gated_block(lambda hd, sl: qs_ref[hd, sl, :], lambda hd, sl: ks_ref[hd, sl, :],
                               lambda hd, sl: i_ref[sl, lanes[hd]], lambda hd, sl: la_ref[hd, sl, :],
                               e_ref, rsel_ref, mask_ref, [st_ref[hd] for hd in range(len(lanes))],
                               nchunks, group, emits)
    for hd, st in enumerate(states):
        st_ref[hd] = st


def _hgrn(proj, lb, norm_g, rsel, masks, bsz, seq, tblk, group):
    nt = seq // tblk
    hps = DIAG_HEADS_PER_STEP
    width = hps * LANES

    def col(base):
        return pl.BlockSpec((tblk, width), lambda b, p, t: (b * nt + t, base // hps + p))

    return pl.pallas_call(
        functools.partial(_hgrn_kernel, nchunks=tblk // CHUNK, group=group),
        grid=(bsz, A_HEADS // hps, nt),
        in_specs=[col(COL_AQ), col(COL_AF), col(COL_AI), col(COL_AG),
                  pl.BlockSpec((hps, 1, LANES), lambda b, p, t: (p, 0, 0)),
                  _full_spec((1, A_DV)), _full_spec(rsel.shape), _full_spec(masks.shape)],
        out_specs=pl.BlockSpec((tblk, width), lambda b, p, t: (b * nt + t, p)),
        out_shape=jax.ShapeDtypeStruct((bsz * seq, A_HEADS * A_DV), BF16),
        scratch_shapes=[pltpu.VMEM((hps, A_DV, HEAD_DK), F32), pltpu.VMEM((hps, tblk, HEAD_DK), F32),
                        pltpu.VMEM((hps, tblk, HEAD_DK), F32), pltpu.VMEM((hps, tblk, HEAD_DK), F32),
                        pltpu.VMEM((hps, tblk * _N_SEL, HEAD_DK), F32)],
        compiler_params=_cparams(("arbitrary", "arbitrary", "arbitrary")),
        name="hgrn2",
    )(proj, proj, proj, proj, lb, norm_g.reshape(1, A_DV), rsel, masks)


def _gla_kernel(q_ref, k_ref, v_ref, r_ref, misc_ref, wup_ref, bup_ref, ng_ref, rsel_ref, mask_ref, o_ref,
                st_ref, qs_ref, la_ref, e_ref, *, nchunks, group):
    @pl.when(pl.program_id(2) == 0)
    def _():
        st_ref[...] = jnp.zeros_like(st_ref)

    ng = ng_ref[...]
    code = misc_ref[...].astype(BF16)
    nh = DIAG_HEADS_PER_STEP
    lanes = [slice(hd * LANES, (hd + 1) * LANES) for hd in range(nh)]
    vlanes = [slice(hd * C_DV, (hd + 1) * C_DV) for hd in range(nh)]
    emits = []
    for hd in range(nh):
        z = _dot(code, wup_ref[hd]) + bup_ref[hd]
        la_ref[hd] = (jnp.minimum(z, 0.0) - jnp.log(1.0 + jnp.exp(-jnp.abs(z)))) * (1.0 / C_TAU)
        qs_ref[hd] = q_ref[:, lanes[hd]] * (HEAD_DK ** -0.5)

        def emit(c, o, vl=vlanes[hd]):
            sl = slice(c * CHUNK, (c + 1) * CHUNK)
            o_ref[sl, vl] = _rms_gate(o, ng, r_ref[sl, vl]).astype(o_ref.dtype)

        emits.append(emit)

    states = _diag_gated_block(lambda hd, sl: qs_ref[hd, sl, :], lambda hd, sl: k_ref[sl, lanes[hd]],
                               lambda hd, sl: v_ref[sl, vlanes[hd]], lambda hd, sl: la_ref[hd, sl, :],
                               e_ref, rsel_ref, mask_ref, [st_ref[hd] for hd in range(nh)],
                               nchunks, group, emits)
    for hd, st in enumerate(states):
        st_ref[hd] = st


def _gla(proj, misc, w_up_pad, b_up, norm_g, rsel, masks, bsz, seq, tblk, group):
    nt = seq // tblk
    hps = DIAG_HEADS_PER_STEP

    def col(base, per_head):
        width = hps * per_head
        return pl.BlockSpec((tblk, width), lambda b, p, t: (b * nt + t, (base * LANES) // width + p))

    return pl.pallas_call(
        functools.partial(_gla_kernel, nchunks=tblk // CHUNK, group=group),
        grid=(bsz, C_HEADS // hps, nt),
        in_specs=[col(COL_CQ, LANES), col(COL_CK, LANES), col(COL_CV, C_DV), col(COL_CR, C_DV),
                  pl.BlockSpec((tblk, LANES), lambda b, p, t: (b * nt + t, 0)),
                  pl.BlockSpec((hps, LANES, LANES), lambda b, p, t: (p, 0, 0)),
                  pl.BlockSpec((hps, 1, LANES), lambda b, p, t: (p, 0, 0)),
                  _full_spec((1, C_DV)), _full_spec(rsel.shape), _full_spec(masks.shape)],
        out_specs=pl.BlockSpec((tblk, hps * C_DV), lambda b, p, t: (b * nt + t, p)),
        out_shape=jax.ShapeDtypeStruct((bsz * seq, C_HEADS * C_DV), BF16),
        scratch_shapes=[pltpu.VMEM((hps, C_DV, HEAD_DK), F32), pltpu.VMEM((hps, tblk, HEAD_DK), F32),
                        pltpu.VMEM((hps, tblk, HEAD_DK), F32), pltpu.VMEM((hps, tblk * _N_SEL, HEAD_DK), F32)],
        compiler_params=_cparams(("arbitrary", "arbitrary", "arbitrary")),
        name="gla",
    )(proj, proj, proj, proj, misc, w_up_pad, b_up, norm_g.reshape(1, C_DV), rsel, masks)


def _times_p(lefts, p):
    n = p.shape[0]
    ph, pl_ = _split(p)
    halves = [_split(x) for x in lefts]
    by_hi = _dot(jnp.concatenate([h for pair in halves for h in pair], axis=0), ph)
    by_lo = _dot(jnp.concatenate([hi for hi, _ in halves], axis=0), pl_)
    return [by_hi[2 * i * n:(2 * i + 1) * n] + by_hi[(2 * i + 1) * n:(2 * i + 2) * n] + by_lo[i * n:(i + 1) * n]
            for i in range(len(lefts))]


def _unit_lower_inverses(neg_ms):
    shape = neg_ms[0].shape
    eye = jnp.where(_iota(shape, 0) == _iota(shape, 1), 1.0, 0.0)
    ts = [eye + m for m in neg_ms]
    ps = [_times_p([m], m)[0] for m in neg_ms]
    for _ in range(4):
        prods = [_times_p([t, p], p) for t, p in zip(ts, ps)]
        ts = [t + tp for t, (tp, _) in zip(ts, prods)]
        ps = [pp for _, pp in prods]
    return [t + _times_p([t], p)[0] for t, p in zip(ts, ps)]


def _gdn_block(buf_ref, states, nchunks, group, emits):
    nh = len(states)
    states = list(states)
    ri = _iota((CHUNK, CHUNK), 0)
    ci = _iota((CHUNK, CHUNK), 1)
    incl = ri >= ci
    strict = ri > ci
    tri = jnp.where(_iota((CHUNK, 2 * CHUNK), 0) >= (_iota((CHUNK, 2 * CHUNK), 1) & (CHUNK - 1)),
                    1.0, 0.0).astype(BF16)
    upper = jnp.where(strict, 1.0, 0.0)
    for g0 in range(0, nchunks, group):
        items = [(hd, c) for c in range(g0, min(g0 + group, nchunks)) for hd in range(nh)]
        sls = [(hd, slice(c * CHUNK, (c + 1) * CHUNK)) for hd, c in items]
        qs = [buf_ref[hd, 3, sl, :] for hd, sl in sls]
        ks = [buf_ref[hd, 4, sl, :] for hd, sl in sls]
        vs = [buf_ref[hd, 5, sl, :] for hd, sl in sls]
        betas = [buf_ref[hd, 6, sl, :] for hd, sl in sls]
        gs = [buf_ref[hd, 7, sl, :] for hd, sl in sls]
        gcs = [_dot_exact_lhs(tri, g) for g in gs]
        dmats = [_dot_exact_lhs(tri, g[:, :CHUNK] * upper) for g in gs]
        gammas = [jnp.where(incl, jnp.exp(jnp.where(incl, dm, 0.0)), 0.0) for dm in dmats]
        qbs = [q.astype(BF16) for q in qs]
        kbs = [k.astype(BF16) for k in ks]
        kks = [_dot_nt(kb, kb) for kb in kbs]
        ts = _unit_lower_inverses([-jnp.where(strict, beta[:, :CHUNK] * kk * gamma, 0.0)
                                   for beta, kk, gamma in zip(betas, kks, gammas)])
        egcs = [jnp.exp(gc) for gc in gcs]
        rhss = [jnp.concatenate([k * (beta * egc), v * beta], axis=1).astype(BF16)
                for k, v, beta, egc in zip(ks, vs, betas, egcs)]
        wus = []
        for t, rhs in zip(ts, rhss):
            both = _dot(jnp.concatenate(_split(t), axis=0), rhs)
            wus.append((both[:CHUNK] + both[CHUNK:]).astype(BF16))
        attns = [jnp.where(incl, _dot_nt(qb, kb) * gamma, 0.0).astype(BF16)
                 for qb, kb, gamma in zip(qbs, kbs, gammas)]
        gls = [gc[CHUNK - 1:CHUNK, :] for gc in gcs]
        k_decs = [(k * jnp.exp(gl - gc)).astype(BF16) for k, gl, gc in zip(ks, gls, gcs)]
        kwus = [_dot_tn(kd, wu) for kd, wu in zip(k_decs, wus)]
        awus = [_dot(a, wu) for a, wu in zip(attns, wus)]
        q_effs = [(q * egc - awu[:, :HEAD_DK]).astype(BF16) for q, egc, awu in zip(qs, egcs, awus)]
        for n, (hd, c) in enumerate(items):
            st_b = states[hd].astype(BF16)
            emits[hd](c, _dot(q_effs[n], st_b) + awus[n][:, HEAD_DK:])
            states[hd] = (states[hd] * jnp.exp(gls[n]) - _dot(kwus[n][:, :HEAD_DK].astype(BF16), st_b)
                          + kwus[n][:, HEAD_DK:])
    return states


def _l2n(x):
    return x * lax.rsqrt(jnp.sum(x * x, axis=-1, keepdims=True) + RMS_EPS)


def _gdn_kernel(q_ref, k_ref, v_ref, z_ref, misc_ref, cw_ref, al_ref, dt_ref, ng_ref, o_ref,
                st_ref, buf_ref, *, nchunks, tblk, group):
    first_head = pl.program_id(1) * GDN_HEADS_PER_STEP

    @pl.when(pl.program_id(2) == 0)
    def _():
        st_ref[...] = jnp.zeros_like(st_ref)
        buf_ref[...] = jnp.zeros_like(buf_ref)

    mh, ml = _split(misc_ref[...])
    lane_r = _iota((LANES, LANES), 0)
    ng = ng_ref[...]
    emits = []
    for hd in range(GDN_HEADS_PER_STEP):
        lanes = slice(hd * LANES, (hd + 1) * LANES)
        conv = []
        for n, ref in enumerate((q_ref, k_ref, v_ref)):
            buf_ref[hd, n, 8:, :] = ref[:, lanes]
            acc = jnp.zeros((tblk, LANES), F32)
            for j in range(B_CONV):
                lo = 8 - (B_CONV - 1) + j
                acc = acc + cw_ref[hd, n, j:j + 1, :] * buf_ref[hd, n, lo:lo + tblk, :]
            buf_ref[hd, n, 0:8, :] = buf_ref[hd, n, tblk:tblk + 8, :]
            conv.append(_silu(acc))
        buf_ref[hd, 3, 0:tblk, :] = _l2n(conv[0]) * (HEAD_DK ** -0.5)
        buf_ref[hd, 4, 0:tblk, :] = _l2n(conv[1])
        buf_ref[hd, 5, 0:tblk, :] = conv[2]

        sel_beta = jnp.where(lane_r == MISC_BETA + first_head + hd, 1.0, 0.0).astype(BF16)
        sel_dec = jnp.where(lane_r == MISC_DECAY + first_head + hd, 1.0, 0.0).astype(BF16)
        beta_logit = _dot(mh, sel_beta) + _dot(ml, sel_beta)
        dec_in = _dot(mh, sel_dec) + _dot(ml, sel_dec) + dt_ref[hd]
        softplus = jnp.maximum(dec_in, 0.0) + jnp.log(1.0 + jnp.exp(-jnp.abs(dec_in)))
        buf_ref[hd, 6, 0:tblk, :] = _sigmoid(beta_logit)
        buf_ref[hd, 7, 0:tblk, :] = -jnp.exp(al_ref[hd]) * softplus

        def emit(c, o, lanes=lanes):
            sl = slice(c * CHUNK, (c + 1) * CHUNK)
            o_ref[sl, lanes] = _rms_gate(o, ng, z_ref[sl, lanes]).astype(o_ref.dtype)

        emits.append(emit)

    states = _gdn_block(buf_ref, [st_ref[hd] for hd in range(GDN_HEADS_PER_STEP)], nchunks, group, emits)
    for hd in range(GDN_HEADS_PER_STEP):
        st_ref[hd] = states[hd]


def _gdn(proj, misc, conv_w, a_log, dt_bias, norm_g, bsz, seq, tblk, group):
    nt = seq // tblk
    hps = GDN_HEADS_PER_STEP
    width = hps * LANES

    def col(base):
        return pl.BlockSpec((tblk, width), lambda b, p, t: (b * nt + t, base // hps + p))

    def per_group(shape):
        return pl.BlockSpec((None, hps) + shape, lambda b, p, t: (p, 0) + (0,) * len(shape))

    return pl.pallas_call(
        functools.partial(_gdn_kernel, nchunks=tblk // CHUNK, tblk=tblk, group=group),
        grid=(bsz, B_HEADS // hps, nt),
        in_specs=[col(COL_BQ), col(COL_BK), col(COL_BV), col(COL_BZ),
                  pl.BlockSpec((tblk, LANES), lambda b, p, t: (b * nt + t, 0)),
                  per_group((3, B_CONV, LANES)), per_group((1, LANES)), per_group((1, LANES)),
                  _full_spec((1, B_DV))],
        out_specs=pl.BlockSpec((tblk, width), lambda b, p, t: (b * nt + t, p)),
        out_shape=jax.ShapeDtypeStruct((bsz * seq, B_HEADS * B_DV), BF16),
        scratch_shapes=[pltpu.VMEM((hps, HEAD_DK, B_DV), F32), pltpu.VMEM((hps, 8, tblk + 8, LANES), F32)],
        compiler_params=_cparams(("arbitrary", "arbitrary", "arbitrary")),
        name="gdn",
    )(proj, proj, proj, proj, misc,
      conv_w.reshape((B_HEADS // hps, hps) + conv_w.shape[1:]),
      a_log.reshape(B_HEADS // hps, hps, 1, LANES), dt_bias.reshape(B_HEADS // hps, hps, 1, LANES),
      norm_g.reshape(1, B_DV))


def _xattn_kernel(xf_ref, wq_ref, kv_ref, wo_ref, g_ref, b_ref, of_ref, ob_ref, q_ref, o_ref, *, alpha):
    tq, d = of_ref.shape
    width = X_HEADS * X_HEAD_DIM
    q_ref[...] = _dot(xf_ref[...].astype(BF16), wq_ref[...]).astype(BF16)
    for hd in range(X_HEADS):
        lo = hd * X_HEAD_DIM
        kh = kv_ref[:, lo:lo + X_HEAD_DIM]
        vh = kv_ref[:, width + lo:width + lo + X_HEAD_DIM]
        for r in range(0, tq, ATTN_ROWS):
            s = _dot_nt(q_ref[r:r + ATTN_ROWS, lo:lo + X_HEAD_DIM], kh) * (X_HEAD_DIM ** -0.5)
            s = s - jnp.max(s, axis=-1, keepdims=True)
            p = jnp.exp(s)
            p = p / jnp.sum(p, axis=-1, keepdims=True)
            o_ref[r:r + ATTN_ROWS, lo:lo + X_HEAD_DIM] = _dot(p.astype(BF16), vh).astype(BF16)
    o = o_ref[...]
    for c in range(0, d, COL_SLICE):
        of_ref[:, c:c + COL_SLICE] = _dot(o, wo_ref[:, c:c + COL_SLICE])
    for r in range(0, tq, ROW_SLICE):
        sl = slice(r, r + ROW_SLICE)
        y = _ln_rows(alpha * xf_ref[sl, :] + of_ref[sl, :], g_ref[...], b_ref[...])
        of_ref[sl, :] = y
        ob_ref[sl, :] = y.astype(BF16)


def _xattn(xf, wq, kv, wo, layer, g, b, alpha, bsz, seq, tq):
    d = xf.shape[1]
    nt = seq // tq
    width = X_HEADS * X_HEAD_DIM
    n_mem = kv.shape[1]
    return pl.pallas_call(
        functools.partial(_xattn_kernel, alpha=alpha),
        grid=(bsz, nt),
        in_specs=[pl.BlockSpec((tq, d), lambda bb, t: (bb * nt + t, 0)),
                  pl.BlockSpec((None, d, width), lambda bb, t: (layer, 0, 0), pipeline_mode=pl.Buffered(1)),
                  pl.BlockSpec((None, n_mem, 2 * width), lambda bb, t: (bb, 0, 0)),
                  pl.BlockSpec((None, width, d), lambda bb, t: (layer, 0, 0), pipeline_mode=pl.Buffered(1)),
                  pl.BlockSpec((1, d), lambda bb, t: (0, 0)),
                  pl.BlockSpec((1, d), lambda bb, t: (0, 0))],
        out_specs=[pl.BlockSpec((tq, d), lambda bb, t: (bb * nt + t, 0)),
                   pl.BlockSpec((tq, d), lambda bb, t: (bb * nt + t, 0))],
        out_shape=[jax.ShapeDtypeStruct((bsz * seq, d), F32), jax.ShapeDtypeStruct((bsz * seq, d), BF16)],
        scratch_shapes=[pltpu.VMEM((tq, width), BF16), pltpu.VMEM((tq, width), BF16)],
        compiler_params=_cparams(("arbitrary", "arbitrary")),
        name="xattn",
    )(xf, wq, kv, wo, g.reshape(1, d), b.reshape(1, d))


def _router_kernel(xb_ref, w_ref, b_ref, o_ref):
    logits = _dot(xb_ref[...], w_ref[...]) + b_ref[...]
    lane = _iota(logits.shape, 1).astype(F32)
    neg = jnp.float32(-3.0e38)
    picked = []
    vals = []
    work = logits
    for _ in range(TOP_K):
        mx = jnp.max(work, axis=-1, keepdims=True)
        idx = jnp.min(jnp.where(work == mx, lane, float(LANES)), axis=-1, keepdims=True)
        sel = lane == idx
        picked.append(sel)
        vals.append(mx)
        work = jnp.where(sel, neg, work)
    es = [jnp.exp(vv - vals[0]) for vv in vals]
    inv = 1.0 / (es[0] + es[1] + es[2] + es[3])
    comb = jnp.zeros_like(logits)
    for sel, e in zip(picked, es):
        comb = comb + jnp.where(sel, e * inv, 0.0)
    o_ref[...] = comb


def _router(xb, w_pad, b_pad, tm):
    m, d = xb.shape
    return pl.pallas_call(
        _router_kernel,
        grid=(m // tm,),
        in_specs=[pl.BlockSpec((tm, d), lambda i: (i, 0)),
                  pl.BlockSpec((d, LANES), lambda i: (0, 0)),
                  pl.BlockSpec((1, LANES), lambda i: (0, 0))],
        out_specs=pl.BlockSpec((tm, LANES), lambda i: (i, 0)),
        out_shape=jax.ShapeDtypeStruct((m, LANES), F32),
        compiler_params=_cparams(("arbitrary",)),
        name="router",
    )(xb, w_pad, b_pad)


def _moe_up_kernel(xb_ref, comb_ref, wg_ref, bg_ref, wu_ref, bu_ref, h_ref):
    e = pl.program_id(1)
    sel = jnp.where(_iota((LANES, EXPERT_FF), 0) == e, 1.0, 0.0).astype(BF16)
    xb = xb_ref[...]
    gate = jnp.minimum(_dot(xb, wg_ref[...].astype(BF16)) + bg_ref[...], SWIGLU_LIMIT)
    lin = jnp.clip(_dot(xb, wu_ref[...].astype(BF16)) + bu_ref[...], -SWIGLU_LIMIT, SWIGLU_LIMIT)
    hid = (lin + 1.0) * gate * _sigmoid(SWIGLU_ALPHA * gate)
    ch, cl = _split(comb_ref[...])
    weight = _dot(ch, sel) + _dot(cl, sel)
    h_ref[...] = (hid * weight).astype(BF16)


def _moe_up(xb, comb, wg, bg, wu, bu, layer, ne, tm):
    m, d = xb.shape
    ff = wg.shape[2]
    base = layer * ne
    return pl.pallas_call(
        _moe_up_kernel,
        grid=(m // tm, ne),
        in_specs=[pl.BlockSpec((tm, d), lambda i, e: (i, 0)),
                  pl.BlockSpec((tm, LANES), lambda i, e: (i, 0)),
                  pl.BlockSpec((None, d, ff), lambda i, e: (base + e, 0, 0)),
                  pl.BlockSpec((None, 1, ff), lambda i, e: (base + e, 0, 0)),
                  pl.BlockSpec((None, d, ff), lambda i, e: (base + e, 0, 0)),
                  pl.BlockSpec((None, 1, ff), lambda i, e: (base + e, 0, 0))],
        out_specs=pl.BlockSpec((tm, ff), lambda i, e: (i, e)),
        out_shape=jax.ShapeDtypeStruct((m, ne * ff), BF16),
        compiler_params=_cparams(("arbitrary", "arbitrary")),
        name="moe_up",
    )(xb, comb, wg, bg, wu, bu)


def _moe_down_kernel(h_ref, wd_ref, comb_ref, bd_ref, xf_ref, g_ref, b_ref, of_ref, ob_ref, *, alpha, nk):
    kk = pl.program_id(1)
    tm, d = of_ref.shape

    @pl.when(kk == 0)
    def _():
        of_ref[...] = jnp.zeros_like(of_ref)

    h = h_ref[...]
    for c in range(0, d, COL_SLICE):
        of_ref[:, c:c + COL_SLICE] += _dot(h, wd_ref[:, c:c + COL_SLICE])

    @pl.when(kk == nk - 1)
    def _():
        for r in range(0, tm, ROW_SLICE):
            sl = slice(r, r + ROW_SLICE)
            y = of_ref[sl, :] + _dot(comb_ref[sl, :].astype(BF16), bd_ref[...])
            y = _ln_rows(alpha * xf_ref[sl, :] + y, g_ref[...], b_ref[...])
            of_ref[sl, :] = y
            ob_ref[sl, :] = y.astype(BF16)


def _moe_down(h, wd, layer, comb, bd_pad, xf, g, b, alpha, tm, tk):
    m, d = xf.shape
    nk = h.shape[1] // tk
    once = pl.Buffered(1)
    return pl.pallas_call(
        functools.partial(_moe_down_kernel, alpha=alpha, nk=nk),
        grid=(m // tm, nk),
        in_specs=[pl.BlockSpec((tm, tk), lambda i, kk: (i, kk)),
                  pl.BlockSpec((None, tk, d), lambda i, kk: (layer, kk, 0)),
                  pl.BlockSpec((tm, LANES), lambda i, kk: (i, 0), pipeline_mode=once),
                  pl.BlockSpec((LANES, d), lambda i, kk: (0, 0), pipeline_mode=once),
                  pl.BlockSpec((tm, d), lambda i, kk: (i, 0), pipeline_mode=once),
                  pl.BlockSpec((1, d), lambda i, kk: (0, 0)),
                  pl.BlockSpec((1, d), lambda i, kk: (0, 0))],
        out_specs=[pl.BlockSpec((tm, d), lambda i, kk: (i, 0)),
                   pl.BlockSpec((tm, d), lambda i, kk: (i, 0))],
        out_shape=[jax.ShapeDtypeStruct((m, d), F32), jax.ShapeDtypeStruct((m, d), BF16)],
        compiler_params=_cparams(("arbitrary", "arbitrary")),
        name="moe_down",
    )(h, wd, comb, bd_pad, xf, g.reshape(1, d), b.reshape(1, d))


def _pick(n, prefs):
    for p in prefs:
        if n % p == 0:
            return p
    raise ValueError(f"no tile in {prefs} divides {n}")


def kernel(x, mem, w_in, hgrn_lb_raw, hgrn_norm_g, gdn_conv_w, gdn_a_log, gdn_dt_bias, gdn_norm_g, gla_w_up, gla_b_up, gla_norm_g, w_out, ln_mix_g, ln_mix_b, mem_ln_g, mem_ln_b, xattn_wq, xattn_wkv, xattn_wo, ln_xattn_g, ln_xattn_b, w_router, b_router, w_gate, b_gate, w_up, b_up, w_down, b_down, ln_ffn_g, ln_ffn_b):
    bsz, seq, d = x.shape
    depth = w_in.shape[0]
    n_mem = mem.shape[1]
    ntok = bsz * seq
    alpha = float((2 * depth) ** 0.25)
    assert seq % CHUNK == 0 and d % LANES == 0

    tblk = _pick(seq, (1024, 512, 256, 128, 64))
    group = min(8, tblk // CHUNK)
    tblk_gdn = _pick(seq, (512, 256, 128, 64))
    tm_big = _pick(ntok, (512, 256, 128, 64))
    tm_mid = _pick(ntok, (256, 128, 64))
    tm_up = _pick(ntok, (1024, 512, 256, 128, 64))

    a_w, b_w, c_w = A_HEADS * HEAD_DK, B_HEADS * HEAD_DK, C_HEADS * HEAD_DK
    o_bb = 4 * a_w + (2 * b_w + B_HEADS * B_DV) + B_HEADS * B_DV
    o_cq = o_bb + 2 * B_HEADS
    o_clr = o_cq + 2 * c_w + 2 * C_HEADS * C_DV
    n_cols = o_clr + C_RANK
    assert w_in.shape[2] == n_cols
    assert o_bb == MAIN_COLS
    w_main = w_in[:, :, :o_bb].astype(BF16)
    w_c = w_in[:, :, o_cq:o_clr].astype(BF16)
    w_misc = jnp.concatenate([w_in[:, :, o_bb:o_cq], w_in[:, :, o_clr:],
                              jnp.zeros((depth, d, LANES - 2 * B_HEADS - C_RANK), w_in.dtype)],
                             axis=2).astype(BF16)
    w_out_b = w_out.astype(BF16)
    wq_b, wkv_b, wo_b = xattn_wq.astype(BF16), xattn_wkv.astype(BF16), xattn_wo.astype(BF16)
    wr_p = jnp.concatenate([w_router, jnp.zeros((depth, d, LANES - N_EXPERTS), F32)], axis=2).astype(BF16)
    br_p = jnp.concatenate([b_router, jnp.full((depth, LANES - N_EXPERTS), -1.0e30, F32)], axis=1)
    ne, ff = w_gate.shape[1], w_gate.shape[3]
    wg_r = w_gate.reshape(depth * ne, d, ff)
    wu_r = w_up.reshape(depth * ne, d, ff)
    wd_b = w_down.astype(BF16).reshape(depth, ne * ff, d)
    bg_r = b_gate.reshape(depth * ne, 1, ff)
    bu_r = b_up.reshape(depth * ne, 1, ff)
    bd_p = jnp.concatenate([b_down, jnp.zeros((depth, LANES - N_EXPERTS, d), F32)], axis=1).astype(BF16)
    gla_wup_p = jnp.zeros((depth, LANES, c_w), F32).at[:, MISC_CODE:MISC_CODE + C_RANK, :].set(gla_w_up)
    gla_wup_p = gla_wup_p.reshape(depth, LANES, C_HEADS, HEAD_DK).transpose(0, 2, 1, 3).astype(BF16)
    gla_bup_r = gla_b_up.reshape(depth, C_HEADS, 1, HEAD_DK)
    conv_r = gdn_conv_w.reshape(depth, B_CONV, 3, B_HEADS, HEAD_DK).transpose(0, 3, 2, 1, 4)
    a_log_r = jnp.broadcast_to(gdn_a_log[:, :, None, None], (depth, B_HEADS, 1, LANES))
    dt_r = jnp.broadcast_to(gdn_dt_bias[:, :, None, None], (depth, B_HEADS, 1, LANES))
    lb_all = jnp.cumsum(jax.nn.softmax(hgrn_lb_raw.astype(F32), axis=0), axis=0)
    lb_r = (lb_all - lb_all[0:1]).reshape(depth, A_HEADS, 1, HEAD_DK)
    rsel = _range_selectors()
    masks = _level_masks()

    tm_mem = _pick(bsz * n_mem, (256, 128, 64, 8))
    mem_n = _layer_norm(mem.reshape(bsz * n_mem, d), mem_ln_g, mem_ln_b, BF16, tm_mem)

    xf = x.reshape(ntok, d)
    xb = xf.astype(BF16)
    for l in range(depth):
        proj = _matmul(xb, w_main, l, F32, tm_big, 1280)
        proj_c = _matmul(xb, w_c, l, F32, tm_big, 1536)
        misc = _matmul(xb, w_misc, l, F32, tm_big, LANES)
        oa = _hgrn(proj, lb_r[l], hgrn_norm_g[l], rsel, masks, bsz, seq, tblk, group)
        ob = _gdn(proj, misc, conv_r[l], a_log_r[l], dt_r[l], gdn_norm_g[l], bsz, seq, tblk_gdn, group)
        oc = _gla(proj_c, misc, gla_wup_p[l], gla_bup_r[l], gla_norm_g[l], rsel, masks, bsz, seq, tblk, group)
        xf, xb = _proj_ln(oa, ob, oc, w_out_b, l, xf, ln_mix_g[l], ln_mix_b[l], alpha, tm_big, 512)

        kv = _matmul(mem_n, wkv_b, l, BF16, tm_mem, 2 * X_HEADS * X_HEAD_DIM)
        xf, xb = _xattn(xf, wq_b, kv.reshape(bsz, n_mem, -1), wo_b, l, ln_xattn_g[l], ln_xattn_b[l],
                        alpha, bsz, seq, _pick(seq, (512, 256, 128, 64)))

        comb = _router(xb, wr_p[l], br_p[l, None, :], tm_big)
        hid = _moe_up(xb, comb, wg_r, bg_r, wu_r, bu_r, l, ne, tm_up)
        xf, xb = _moe_down(hid, wd_b, l, comb, bd_p[l], xf, ln_ffn_g[l], ln_ffn_b[l], alpha, tm_big, 1024)
    return xf.reshape(bsz, seq, d)
```

```python
import functools

import numpy as np
import jax
import jax.numpy as jnp
from jax import lax
from jax.experimental import pallas as pl
from jax.experimental.pallas import tpu as pltpu

F32 = jnp.float32
BF16 = jnp.bfloat16

LANES = 128
CHUNK = 64
VMEM_LIMIT = 62 * 1024 * 1024
ROW_SLICE = 256
COL_SLICE = 1024
ATTN_ROWS = 128

A_HEADS, B_HEADS, C_HEADS = 8, 12, 6
HEAD_DK = 128
A_DV, B_DV, C_DV = 128, 128, 256
B_CONV = 4
GDN_HEADS_PER_STEP = 4
DIAG_HEADS_PER_STEP = 2
C_RANK = 16
C_TAU = 16.0
X_HEADS, X_HEAD_DIM = 4, 128
N_EXPERTS, TOP_K, EXPERT_FF = 32, 4, 256
SWIGLU_LIMIT, SWIGLU_ALPHA = 7.0, 1.702
LN_EPS, RMS_EPS = 1e-5, 1e-6

COL_AQ, COL_AF, COL_AI, COL_AG = 0, 8, 16, 24
COL_BQ, COL_BK, COL_BV, COL_BZ = 32, 44, 56, 68
MAIN_COLS = 80 * LANES
COL_CQ, COL_CK, COL_CV, COL_CR = 0, 6, 12, 24
MISC_BETA, MISC_DECAY, MISC_CODE = 0, 12, 24


def _cparams(sem):
    return pltpu.CompilerParams(dimension_semantics=sem, vmem_limit_bytes=VMEM_LIMIT)


def _dot(a, b):
    return jnp.dot(a, b, preferred_element_type=F32)


def _dot_nt(a, b):
    return lax.dot_general(a, b, (((1,), (1,)), ((), ())), preferred_element_type=F32)


def _dot_tn(a, b):
    return lax.dot_general(a, b, (((0,), (0,)), ((), ())), preferred_element_type=F32)


def _split(x):
    hi = x.astype(BF16)
    lo = (x - hi.astype(F32)).astype(BF16)
    return hi, lo


def _dot_exact_lhs(sel2, x):
    hi, lo = _split(x)
    return _dot(sel2, jnp.concatenate([hi, lo], axis=0))


def _sigmoid(x):
    return 1.0 / (1.0 + jnp.exp(-x))


def _silu(x):
    return x * _sigmoid(x)


def _iota(shape, dim):
    return lax.broadcasted_iota(jnp.int32, shape, dim)


def _full_spec(shape):
    return pl.BlockSpec(shape, lambda *_: (0,) * len(shape))


def _mm_kernel(a_ref, w_ref, o_ref):
    o_ref[...] = _dot(a_ref[...], w_ref[...]).astype(o_ref.dtype)


def _matmul(a, w, layer, out_dtype, tm, tn):
    m, k = a.shape
    n = w.shape[2]
    return pl.pallas_call(
        _mm_kernel,
        grid=(n // tn, m // tm),
        in_specs=[pl.BlockSpec((tm, k), lambda j, i: (i, 0)),
                  pl.BlockSpec((None, k, tn), lambda j, i: (layer, 0, j))],
        out_specs=pl.BlockSpec((tm, tn), lambda j, i: (i, j)),
        out_shape=jax.ShapeDtypeStruct((m, n), out_dtype),
        compiler_params=_cparams(("arbitrary", "arbitrary")),
        name="matmul",
    )(a, w)


def _ln_rows(y, g, b):
    mu = jnp.mean(y, axis=-1, keepdims=True)
    d = y - mu
    var = jnp.mean(d * d, axis=-1, keepdims=True)
    return d * lax.rsqrt(var + LN_EPS) * g + b


def _ln_kernel(x_ref, g_ref, b_ref, o_ref):
    o_ref[...] = _ln_rows(x_ref[...], g_ref[...], b_ref[...]).astype(o_ref.dtype)


def _layer_norm(x, g, b, out_dtype, tm):
    m, d = x.shape
    return pl.pallas_call(
        _ln_kernel,
        grid=(m // tm,),
        in_specs=[pl.BlockSpec((tm, d), lambda i: (i, 0)),
                  pl.BlockSpec((1, d), lambda i: (0, 0)),
                  pl.BlockSpec((1, d), lambda i: (0, 0))],
        out_specs=pl.BlockSpec((tm, d), lambda i: (i, 0)),
        out_shape=jax.ShapeDtypeStruct((m, d), out_dtype),
        compiler_params=_cparams(("arbitrary",)),
        name="layer_norm",
    )(x, g.reshape(1, d), b.reshape(1, d))


def _proj_ln_kernel(a0_ref, a1_ref, a2_ref, w_ref, x_ref, g_ref, b_ref, of_ref, ob_ref, *, alpha, bounds):
    kk = pl.program_id(1)
    n0, n1, nk = bounds
    tm, d = of_ref.shape
    rows = [slice(r, r + ROW_SLICE) for r in range(0, tm, ROW_SLICE)]
    cols = [slice(c, c + COL_SLICE) for c in range(0, d, COL_SLICE)]

    @pl.when(kk == 0)
    def _():
        of_ref[...] = jnp.zeros_like(of_ref)

    def accumulate(a_ref):
        a = a_ref[...]
        for sl in cols:
            of_ref[:, sl] += _dot(a, w_ref[:, sl])

    pl.when(kk < n0)(lambda: accumulate(a0_ref))
    pl.when((kk >= n0) & (kk < n1))(lambda: accumulate(a1_ref))
    pl.when(kk >= n1)(lambda: accumulate(a2_ref))

    @pl.when(kk == nk - 1)
    def _():
        for sl in rows:
            y = _ln_rows(alpha * x_ref[sl, :] + of_ref[sl, :], g_ref[...], b_ref[...])
            of_ref[sl, :] = y
            ob_ref[sl, :] = y.astype(BF16)


def _proj_ln(a0, a1, a2, w, layer, x, g, b, alpha, tm, tk):
    m = a0.shape[0]
    d = w.shape[2]
    n0 = a0.shape[1] // tk
    n1 = n0 + a1.shape[1] // tk
    nk = n1 + a2.shape[1] // tk
    assert nk * tk == w.shape[1]
    return pl.pallas_call(
        functools.partial(_proj_ln_kernel, alpha=alpha, bounds=(n0, n1, nk)),
        grid=(m // tm, nk),
        in_specs=[pl.BlockSpec((tm, tk), lambda i, kk: (i, jnp.minimum(kk, n0 - 1))),
                  pl.BlockSpec((tm, tk), lambda i, kk: (i, jnp.clip(kk - n0, 0, n1 - n0 - 1))),
                  pl.BlockSpec((tm, tk), lambda i, kk: (i, jnp.clip(kk - n1, 0, nk - n1 - 1))),
                  pl.BlockSpec((None, tk, d), lambda i, kk: (layer, kk, 0)),
                  pl.BlockSpec((tm, d), lambda i, kk: (i, 0)),
                  pl.BlockSpec((1, d), lambda i, kk: (0, 0)),
                  pl.BlockSpec((1, d), lambda i, kk: (0, 0))],
        out_specs=[pl.BlockSpec((tm, d), lambda i, kk: (i, 0)),
                   pl.BlockSpec((tm, d), lambda i, kk: (i, 0))],
        out_shape=[jax.ShapeDtypeStruct((m, d), F32), jax.ShapeDtypeStruct((m, d), BF16)],
        compiler_params=_cparams(("arbitrary", "arbitrary")),
        name="proj_ln",
    )(a0, a1, a2, w, x, g.reshape(1, d), b.reshape(1, d))


_LEVELS = (32, 16, 8, 4, 2, 1)
_N_SEL = len(_LEVELS) + 2


def _range_selectors():
    i = np.arange(CHUNK)[:, None]
    t = np.arange(CHUNK)[None, :]
    blocks = [t <= i]
    for s in _LEVELS:
        m = (i & ~(2 * s - 1)) | (s - 1)
        later = (i & s) != 0
        blocks.append(np.where(later, (t > m) & (t <= i), (t > i) & (t <= m)))
    blocks.append(t > i)
    sel = np.concatenate(blocks, axis=0).astype(np.float32)
    return jnp.asarray(np.concatenate([sel, sel], axis=1), dtype=BF16)


def _level_masks():
    i = np.arange(CHUNK)[:, None]
    j = np.arange(CHUNK)[None, :]
    masks = [i == j]
    for s in _LEVELS:
        sh = s.bit_length()
        masks.append(((i >> sh) == (j >> sh)) & ((i & s) != 0) & ((j & s) == 0))
    return jnp.asarray(np.stack(masks).astype(np.float32))


def _diag_gated_block(get_q, get_k, get_v, get_la, e_ref, rsel_ref, mask_ref, states_t, nchunks, group, emits):
    nh = len(states_t)
    states_t = list(states_t)
    dk = e_ref.shape[2]
    row = _iota((CHUNK, dk), 0)
    laters = [(row & s) != 0 for s in _LEVELS]
    rsel = rsel_ref[...]
    span = _N_SEL * CHUNK
    for g0 in range(0, nchunks, group):
        items = [(hd, c) for c in range(g0, min(g0 + group, nchunks)) for hd in range(nh)]
        rows = [slice(c * CHUNK, (c + 1) * CHUNK) for _, c in items]
        for (hd, c), sl in zip(items, rows):
            e_ref[hd, c * span:(c + 1) * span, :] = _dot_exact_lhs(rsel, get_la(hd, sl))

        def e_rows(hd, c, first, last):
            return e_ref[hd, c * span + first:c * span + last, :]

        qs = [get_q(hd, sl) for (hd, _), sl in zip(items, rows)]
        ks = [get_k(hd, sl) for (hd, _), sl in zip(items, rows)]
        vbs = [get_v(hd, sl).astype(BF16) for (hd, _), sl in zip(items, rows)]
        scores = [_dot_nt(q.astype(BF16), k.astype(BF16)) * mask_ref[0] for q, k in zip(qs, ks)]
        for n in range(len(_LEVELS)):
            zs = [(jnp.where(laters[n], q, k)
                   * jnp.exp(e_rows(hd, c, (n + 1) * CHUNK, (n + 2) * CHUNK))).astype(BF16)
                  for (hd, c), q, k in zip(items, qs, ks)]
            scores = [sc + _dot_nt(z, z) * mask_ref[n + 1] for sc, z in zip(scores, zs)]
        o_intra = [_dot(sc.astype(BF16), vb) for sc, vb in zip(scores, vbs)]
        q_dec = [(q * jnp.exp(e_rows(hd, c, 0, CHUNK))).astype(BF16) for (hd, c), q in zip(items, qs)]
        k_dec = [(k * jnp.exp(e_rows(hd, c, 7 * CHUNK, span))).astype(BF16) for (hd, c), k in zip(items, ks)]
        ds_t = [_dot_tn(vb, kd) for vb, kd in zip(vbs, k_dec)]
        for n, (hd, c) in enumerate(items):
            emits[hd](c, o_intra[n] + _dot_nt(q_dec[n], states_t[hd].astype(BF16)))
            decay = jnp.exp(e_rows(hd, c, CHUNK - 1, CHUNK))
            states_t[hd] = states_t[hd] * decay + ds_t[n]
    return states_t


def _rms_gate(o, g, gate):
    y = o * lax.rsqrt(jnp.mean(o * o, axis=-1, keepdims=True) + RMS_EPS)
    return y * g * _silu(gate)


def _hgrn_kernel(q_ref, f_ref, i_ref, g_ref, lb_ref, ng_ref, rsel_ref, mask_ref, o_ref,
                 st_ref, qs_ref, ks_ref, la_ref, e_ref, *, nchunks, group):
    @pl.when(pl.program_id(2) == 0)
    def _():
        st_ref[...] = jnp.zeros_like(st_ref)

    ng = ng_ref[...]
    lanes = [slice(hd * LANES, (hd + 1) * LANES) for hd in range(DIAG_HEADS_PER_STEP)]
    emits = []
    for hd, ln in enumerate(lanes):
        lb = lb_ref[hd]
        f = lb + (1.0 - lb) * _sigmoid(f_ref[:, ln])
        qs_ref[hd] = _silu(q_ref[:, ln])
        ks_ref[hd] = 1.0 - f
        la_ref[hd] = jnp.log(f)

        def emit(c, o, ln=ln):
            sl = slice(c * CHUNK, (c + 1) * CHUNK)
            o_ref[sl, ln] = _rms_gate(o, ng, g_ref[sl, ln]).astype(o_ref.dtype)

        emits.append(emit)

    states = _diag_gated_block(lambda hd, sl: qs_ref[hd, sl, :], lambda hd, sl: ks_ref[hd, sl, :],
                               lambda hd, sl: i_ref[sl, lanes[hd]], lambda hd, sl: la_ref[hd, sl, :],
                               e_ref, rsel_ref, mask_ref, [st_ref[hd] for hd in range(len(lanes))],
                               nchunks, group, emits)
    for hd, st in enumerate(states):
        st_ref[hd] = st


def _hgrn(proj, lb, norm_g, rsel, masks, bsz, seq, tblk, group):
    nt = seq // tblk
    hps = DIAG_HEADS_PER_STEP
    width = hps * LANES

    def col(base):
        return pl.BlockSpec((tblk, width), lambda b, p, t: (b * nt + t, base // hps + p))

    return pl.pallas_call(
        functools.partial(_hgrn_kernel, nchunks=tblk // CHUNK, group=group),
        grid=(bsz, A_HEADS // hps, nt),
        in_specs=[col(COL_AQ), col(COL_AF), col(COL_AI), col(COL_AG),
                  pl.BlockSpec((hps, 1, LANES), lambda b, p, t: (p, 0, 0)),
                  _full_spec((1, A_DV)), _full_spec(rsel.shape), _full_spec(masks.shape)],
        out_specs=pl.BlockSpec((tblk, width), lambda b, p, t: (b * nt + t, p)),
        out_shape=jax.ShapeDtypeStruct((bsz * seq, A_HEADS * A_DV), BF16),
        scratch_shapes=[pltpu.VMEM((hps, A_DV, HEAD_DK), F32), pltpu.VMEM((hps, tblk, HEAD_DK), F32),
                        pltpu.VMEM((hps, tblk, HEAD_DK), F32), pltpu.VMEM((hps, tblk, HEAD_DK), F32),
                        pltpu.VMEM((hps, tblk * _N_SEL, HEAD_DK), F32)],
        compiler_params=_cparams(("arbitrary", "arbitrary", "arbitrary")),
        name="hgrn2",
    )(proj, proj, proj, proj, lb, norm_g.reshape(1, A_DV), rsel, masks)


def _gla_kernel(q_ref, k_ref, v_ref, r_ref, misc_ref, wup_ref, bup_ref, ng_ref, rsel_ref, mask_ref, o_ref,
                st_ref, qs_ref, la_ref, e_ref, *, nchunks, group):
    @pl.when(pl.program_id(2) == 0)
    def _():
        st_ref[...] = jnp.zeros_like(st_ref)

    ng = ng_ref[...]
    code = misc_ref[...].astype(BF16)
    nh = DIAG_HEADS_PER_STEP
    lanes = [slice(hd * LANES, (hd + 1) * LANES) for hd in range(nh)]
    vlanes = [slice(hd * C_DV, (hd + 1) * C_DV) for hd in range(nh)]
    emits = []
    for hd in range(nh):
        z = _dot(code, wup_ref[hd]) + bup_ref[hd]
        la_ref[hd] = (jnp.minimum(z, 0.0) - jnp.log(1.0 + jnp.exp(-jnp.abs(z)))) * (1.0 / C_TAU)
        qs_ref[hd] = q_ref[:, lanes[hd]] * (HEAD_DK ** -0.5)

        def emit(c, o, vl=vlanes[hd]):
            sl = slice(c * CHUNK, (c + 1) * CHUNK)
            o_ref[sl, vl] = _rms_gate(o, ng, r_ref[sl, vl]).astype(o_ref.dtype)

        emits.append(emit)

    states = _diag_gated_block(lambda hd, sl: qs_ref[hd, sl, :], lambda hd, sl: k_ref[sl, lanes[hd]],
                               lambda hd, sl: v_ref[sl, vlanes[hd]], lambda hd, sl: la_ref[hd, sl, :],
                               e_ref, rsel_ref, mask_ref, [st_ref[hd] for hd in range(nh)],
                               nchunks, group, emits)
    for hd, st in enumerate(states):
        st_ref[hd] = st


def _gla(proj, misc, w_up_pad, b_up, norm_g, rsel, masks, bsz, seq, tblk, group):
    nt = seq // tblk
    hps = DIAG_HEADS_PER_STEP

    def col(base, per_head):
        width = hps * per_head
        return pl.BlockSpec((tblk, width), lambda b, p, t: (b * nt + t, (base * LANES) // width + p))

    return pl.pallas_call(
        functools.partial(_gla_kernel, nchunks=tblk // CHUNK, group=group),
        grid=(bsz, C_HEADS // hps, nt),
        in_specs=[col(COL_CQ, LANES), col(COL_CK, LANES), col(COL_CV, C_DV), col(COL_CR, C_DV),
                  pl.BlockSpec((tblk, LANES), lambda b, p, t: (b * nt + t, 0)),
                  pl.BlockSpec((hps, LANES, LANES), lambda b, p, t: (p, 0, 0)),
                  pl.BlockSpec((hps, 1, LANES), lambda b, p, t: (p, 0, 0)),
                  _full_spec((1, C_DV)), _full_spec(rsel.shape), _full_spec(masks.shape)],
        out_specs=pl.BlockSpec((tblk, hps * C_DV), lambda b, p, t: (b * nt + t, p)),
        out_shape=jax.ShapeDtypeStruct((bsz * seq, C_HEADS * C_DV), BF16),
        scratch_shapes=[pltpu.VMEM((hps, C_DV, HEAD_DK), F32), pltpu.VMEM((hps, tblk, HEAD_DK), F32),
                        pltpu.VMEM((hps, tblk, HEAD_DK), F32), pltpu.VMEM((hps, tblk * _N_SEL, HEAD_DK), F32)],
        compiler_params=_cparams(("arbitrary", "arbitrary", "arbitrary")),
        name="gla",
    )(proj, proj, proj, proj, misc, w_up_pad, b_up, norm_g.reshape(1, C_DV), rsel, masks)


def _times_p(lefts, p):
    n = p.shape[0]
    ph, pl_ = _split(p)
    halves = [_split(x) for x in lefts]
    by_hi = _dot(jnp.concatenate([h for pair in halves for h in pair], axis=0), ph)
    by_lo = _dot(jnp.concatenate([hi for hi, _ in halves], axis=0), pl_)
    return [by_hi[2 * i * n:(2 * i + 1) * n] + by_hi[(2 * i + 1) * n:(2 * i + 2) * n] + by_lo[i * n:(i + 1) * n]
            for i in range(len(lefts))]


def _unit_lower_inverses(neg_ms):
    shape = neg_ms[0].shape
    eye = jnp.where(_iota(shape, 0) == _iota(shape, 1), 1.0, 0.0)
    ts = [eye + m for m in neg_ms]
    ps = [_times_p([m], m)[0] for m in neg_ms]
    for _ in range(4):
        prods = [_times_p([t, p], p) for t, p in zip(ts, ps)]
        ts = [t + tp for t, (tp, _) in zip(ts, prods)]
        ps = [pp for _, pp in prods]
    return [t + _times_p([t], p)[0] for t, p in zip(ts, ps)]


def _gdn_block(buf_ref, states, nchunks, group, emits):
    nh = len(states)
    states = list(states)
    ri = _iota((CHUNK, CHUNK), 0)
    ci = _iota((CHUNK, CHUNK), 1)
    incl = ri >= ci
    strict = ri > ci
    tri = jnp.where(_iota((CHUNK, 2 * CHUNK), 0) >= (_iota((CHUNK, 2 * CHUNK), 1) & (CHUNK - 1)),
                    1.0, 0.0).astype(BF16)
    upper = jnp.where(strict, 1.0, 0.0)
    for g0 in range(0, nchunks, group):
        items = [(hd, c) for c in range(g0, min(g0 + group, nchunks)) for hd in range(nh)]
        sls = [(hd, slice(c * CHUNK, (c + 1) * CHUNK)) for hd, c in items]
        qs = [buf_ref[hd, 3, sl, :] for hd, sl in sls]
        ks = [buf_ref[hd, 4, sl, :] for hd, sl in sls]
        vs = [buf_ref[hd, 5, sl, :] for hd, sl in sls]
        betas = [buf_ref[hd, 6, sl, :] for hd, sl in sls]
        gs = [buf_ref[hd, 7, sl, :] for hd, sl in sls]
        gcs = [_dot_exact_lhs(tri, g) for g in gs]
        dmats = [_dot_exact_lhs(tri, g[:, :CHUNK] * upper) for g in gs]
        gammas = [jnp.where(incl, jnp.exp(jnp.where(incl, dm, 0.0)), 0.0) for dm in dmats]
        qbs = [q.astype(BF16) for q in qs]
        kbs = [k.astype(BF16) for k in ks]
        kks = [_dot_nt(kb, kb) for kb in kbs]
        ts = _unit_lower_inverses([-jnp.where(strict, beta[:, :CHUNK] * kk * gamma, 0.0)
                                   for beta, kk, gamma in zip(betas, kks, gammas)])
        egcs = [jnp.exp(gc) for gc in gcs]
        rhss = [jnp.concatenate([k * (beta * egc), v * beta], axis=1).astype(BF16)
                for k, v, beta, egc in zip(ks, vs, betas, egcs)]
        wus = []
        for t, rhs in zip(ts, rhss):
            both = _dot(jnp.concatenate(_split(t), axis=0), rhs)
            wus.append((both[:CHUNK] + both[CHUNK:]).astype(BF16))
        attns = [jnp.where(incl, _dot_nt(qb, kb) * gamma, 0.0).astype(BF16)
                 for qb, kb, gamma in zip(qbs, kbs, gammas)]
        gls = [gc[CHUNK - 1:CHUNK, :] for gc in gcs]
        k_decs = [(k * jnp.exp(gl - gc)).astype(BF16) for k, gl, gc in zip(ks, gls, gcs)]
        kwus = [_dot_tn(kd, wu) for kd, wu in zip(k_decs, wus)]
        awus = [_dot(a, wu) for a, wu in zip(attns, wus)]
        q_effs = [(q * egc - awu[:, :HEAD_DK]).astype(BF16) for q, egc, awu in zip(qs, egcs, awus)]
        for n, (hd, c) in enumerate(items):
            st_b = states[hd].astype(BF16)
            emits[hd](c, _dot(q_effs[n], st_b) + awus[n][:, HEAD_DK:])
            states[hd] = (states[hd] * jnp.exp(gls[n]) - _dot(kwus[n][:, :HEAD_DK].astype(BF16), st_b)
                          + kwus[n][:, HEAD_DK:])
    return states


def _l2n(x):
    return x * lax.rsqrt(jnp.sum(x * x, axis=-1, keepdims=True) + RMS_EPS)


def _gdn_kernel(q_ref, k_ref, v_ref, z_ref, misc_ref, cw_ref, al_ref, dt_ref, ng_ref, o_ref,
                st_ref, buf_ref, *, nchunks, tblk, group):
    first_head = pl.program_id(1) * GDN_HEADS_PER_STEP

    @pl.when(pl.program_id(2) == 0)
    def _():
        st_ref[...] = jnp.zeros_like(st_ref)
        buf_ref[...] = jnp.zeros_like(buf_ref)

    mh, ml = _split(misc_ref[...])
    lane_r = _iota((LANES, LANES), 0)
    ng = ng_ref[...]
    emits = []
    for hd in range(GDN_HEADS_PER_STEP):
        lanes = slice(hd * LANES, (hd + 1) * LANES)
        conv = []
        for n, ref in enumerate((q_ref, k_ref, v_ref)):
            buf_ref[hd, n, 8:, :] = ref[:, lanes]
            acc = jnp.zeros((tblk, LANES), F32)
            for j in range(B_CONV):
                lo = 8 - (B_CONV - 1) + j
                acc = acc + cw_ref[hd, n, j:j + 1, :] * buf_ref[hd, n, lo:lo + tblk, :]
            buf_ref[hd, n, 0:8, :] = buf_ref[hd, n, tblk:tblk + 8, :]
            conv.append(_silu(acc))
        buf_ref[hd, 3, 0:tblk, :] = _l2n(conv[0]) * (HEAD_DK ** -0.5)
        buf_ref[hd, 4, 0:tblk, :] = _l2n(conv[1])
        buf_ref[hd, 5, 0:tblk, :] = conv[2]

        sel_beta = jnp.where(lane_r == MISC_BETA + first_head + hd, 1.0, 0.0).astype(BF16)
        sel_dec = jnp.where(lane_r == MISC_DECAY + first_head + hd, 1.0, 0.0).astype(BF16)
        beta_logit = _dot(mh, sel_beta) + _dot(ml, sel_beta)
        dec_in = _dot(mh, sel_dec) + _dot(ml, sel_dec) + dt_ref[hd]
        softplus = jnp.maximum(dec_in, 0.0) + jnp.log(1.0 + jnp.exp(-jnp.abs(dec_in)))
        buf_ref[hd, 6, 0:tblk, :] = _sigmoid(beta_logit)
        buf_ref[hd, 7, 0:tblk, :] = -jnp.exp(al_ref[hd]) * softplus

        def emit(c, o, lanes=lanes):
            sl = slice(c * CHUNK, (c + 1) * CHUNK)
            o_ref[sl, lanes] = _rms_gate(o, ng, z_ref[sl, lanes]).astype(o_ref.dtype)

        emits.append(emit)

    states = _gdn_block(buf_ref, [st_ref[hd] for hd in range(GDN_HEADS_PER_STEP)], nchunks, group, emits)
    for hd in range(GDN_HEADS_PER_STEP):
        st_ref[hd] = states[hd]


def _gdn(proj, misc, conv_w, a_log, dt_bias, norm_g, bsz, seq, tblk, group):
    nt = seq // tblk
    hps = GDN_HEADS_PER_STEP
    width = hps * LANES

    def col(base):
        return pl.BlockSpec((tblk, width), lambda b, p, t: (b * nt + t, base // hps + p))

    def per_group(shape):
        return pl.BlockSpec((None, hps) + shape, lambda b, p, t: (p, 0) + (0,) * len(shape))

    return pl.pallas_call(
        functools.partial(_gdn_kernel, nchunks=tblk // CHUNK, tblk=tblk, group=group),
        grid=(bsz, B_HEADS // hps, nt),
        in_specs=[col(COL_BQ), col(COL_BK), col(COL_BV), col(COL_BZ),
                  pl.BlockSpec((tblk, LANES), lambda b, p, t: (b * nt + t, 0)),
                  per_group((3, B_CONV, LANES)), per_group((1, LANES)), per_group((1, LANES)),
                  _full_spec((1, B_DV))],
        out_specs=pl.BlockSpec((tblk, width), lambda b, p, t: (b * nt + t, p)),
        out_shape=jax.ShapeDtypeStruct((bsz * seq, B_HEADS * B_DV), BF16),
        scratch_shapes=[pltpu.VMEM((hps, HEAD_DK, B_DV), F32), pltpu.VMEM((hps, 8, tblk + 8, LANES), F32)],
        compiler_params=_cparams(("arbitrary", "arbitrary", "arbitrary")),
        name="gdn",
    )(proj, proj, proj, proj, misc,
      conv_w.reshape((B_HEADS // hps, hps) + conv_w.shape[1:]),
      a_log.reshape(B_HEADS // hps, hps, 1, LANES), dt_bias.reshape(B_HEADS // hps, hps, 1, LANES),
      norm_g.reshape(1, B_DV))


def _xattn_kernel(xf_ref, wq_ref, kv_ref, wo_ref, g_ref, b_ref, of_ref, ob_ref, q_ref, o_ref, *, alpha):
    tq, d = of_ref.shape
    width = X_HEADS * X_HEAD_DIM
    q_ref[...] = _dot(xf_ref[...].astype(BF16), wq_ref[...]).astype(BF16)
    for hd in range(X_HEADS):
        lo = hd * X_HEAD_DIM
        kh = kv_ref[:, lo:lo + X_HEAD_DIM]
        vh = kv_ref[:, width + lo:width + lo + X_HEAD_DIM]
        for r in range(0, tq, ATTN_ROWS):
            s = _dot_nt(q_ref[r:r + ATTN_ROWS, lo:lo + X_HEAD_DIM], kh) * (X_HEAD_DIM ** -0.5)
            s = s - jnp.max(s, axis=-1, keepdims=True)
            p = jnp.exp(s)
            p = p / jnp.sum(p, axis=-1, keepdims=True)
            o_ref[r:r + ATTN_ROWS, lo:lo + X_HEAD_DIM] = _dot(p.astype(BF16), vh).astype(BF16)
    o = o_ref[...]
    for c in range(0, d, COL_SLICE):
        of_ref[:, c:c + COL_SLICE] = _dot(o, wo_ref[:, c:c + COL_SLICE])
    for r in range(0, tq, ROW_SLICE):
        sl = slice(r, r + ROW_SLICE)
        y = _ln_rows(alpha * xf_ref[sl, :] + of_ref[sl, :], g_ref[...], b_ref[...])
        of_ref[sl, :] = y
        ob_ref[sl, :] = y.astype(BF16)


def _xattn(xf, wq, kv, wo, layer, g, b, alpha, bsz, seq, tq):
    d = xf.shape[1]
    nt = seq // tq
    width = X_HEADS * X_HEAD_DIM
    n_mem = kv.shape[1]
    return pl.pallas_call(
        functools.partial(_xattn_kernel, alpha=alpha),
        grid=(bsz, nt),
        in_specs=[pl.BlockSpec((tq, d), lambda bb, t: (bb * nt + t, 0)),
                  pl.BlockSpec((None, d, width), lambda bb, t: (layer, 0, 0), pipeline_mode=pl.Buffered(1)),
                  pl.BlockSpec((None, n_mem, 2 * width), lambda bb, t: (bb, 0, 0)),
                  pl.BlockSpec((None, width, d), lambda bb, t: (layer, 0, 0), pipeline_mode=pl.Buffered(1)),
                  pl.BlockSpec((1, d), lambda bb, t: (0, 0)),
                  pl.BlockSpec((1, d), lambda bb, t: (0, 0))],
        out_specs=[pl.BlockSpec((tq, d), lambda bb, t: (bb * nt + t, 0)),
                   pl.BlockSpec((tq, d), lambda bb, t: (bb * nt + t, 0))],
        out_shape=[jax.ShapeDtypeStruct((bsz * seq, d), F32), jax.ShapeDtypeStruct((bsz * seq, d), BF16)],
        scratch_shapes=[pltpu.VMEM((tq, width), BF16), pltpu.VMEM((tq, width), BF16)],
        compiler_params=_cparams(("arbitrary", "arbitrary")),
        name="xattn",
    )(xf, wq, kv, wo, g.reshape(1, d), b.reshape(1, d))


def _router_kernel(xb_ref, w_ref, b_ref, o_ref):
    logits = _dot(xb_ref[...], w_ref[...]) + b_ref[...]
    lane = _iota(logits.shape, 1).astype(F32)
    neg = jnp.float32(-3.0e38)
    picked = []
    vals = []
    work = logits
    for _ in range(TOP_K):
        mx = jnp.max(work, axis=-1, keepdims=True)
        idx = jnp.min(jnp.where(work == mx, lane, float(LANES)), axis=-1, keepdims=True)
        sel = lane == idx
        picked.append(sel)
        vals.append(mx)
        work = jnp.where(sel, neg, work)
    es = [jnp.exp(vv - vals[0]) for vv in vals]
    inv = 1.0 / (es[0] + es[1] + es[2] + es[3])
    comb = jnp.zeros_like(logits)
    for sel, e in zip(picked, es):
        comb = comb + jnp.where(sel, e * inv, 0.0)
    o_ref[...] = comb


def _router(xb, w_pad, b_pad, tm):
    m, d = xb.shape
    return pl.pallas_call(
        _router_kernel,
        grid=(m // tm,),
        in_specs=[pl.BlockSpec((tm, d), lambda i: (i, 0)),
                  pl.BlockSpec((d, LANES), lambda i: (0, 0)),
                  pl.BlockSpec((1, LANES), lambda i: (0, 0))],
        out_specs=pl.BlockSpec((tm, LANES), lambda i: (i, 0)),
        out_shape=jax.ShapeDtypeStruct((m, LANES), F32),
        compiler_params=_cparams(("arbitrary",)),
        name="router",
    )(xb, w_pad, b_pad)


def _moe_up_kernel(xb_ref, comb_ref, wg_ref, bg_ref, wu_ref, bu_ref, h_ref):
    e = pl.program_id(1)
    sel = jnp.where(_iota((LANES, EXPERT_FF), 0) == e, 1.0, 0.0).astype(BF16)
    xb = xb_ref[...]
    gate = jnp.minimum(_dot(xb, wg_ref[...].astype(BF16)) + bg_ref[...], SWIGLU_LIMIT)
    lin = jnp.clip(_dot(xb, wu_ref[...].astype(BF16)) + bu_ref[...], -SWIGLU_LIMIT, SWIGLU_LIMIT)
    hid = (lin + 1.0) * gate * _sigmoid(SWIGLU_ALPHA * gate)
    ch, cl = _split(comb_ref[...])
    weight = _dot(ch, sel) + _dot(cl, sel)
    h_ref[...] = (hid * weight).astype(BF16)


def _moe_up(xb, comb, wg, bg, wu, bu, layer, ne, tm):
    m, d = xb.shape
    ff = wg.shape[2]
    base = layer * ne
    return pl.pallas_call(
        _moe_up_kernel,
        grid=(m // tm, ne),
        in_specs=[pl.BlockSpec((tm, d), lambda i, e: (i, 0)),
                  pl.BlockSpec((tm, LANES), lambda i, e: (i, 0)),
                  pl.BlockSpec((None, d, ff), lambda i, e: (base + e, 0, 0)),
                  pl.BlockSpec((None, 1, ff), lambda i, e: (base + e, 0, 0)),
                  pl.BlockSpec((None, d, ff), lambda i, e: (base + e, 0, 0)),
                  pl.BlockSpec((None, 1, ff), lambda i, e: (base + e, 0, 0))],
        out_specs=pl.BlockSpec((tm, ff), lambda i, e: (i, e)),
        out_shape=jax.ShapeDtypeStruct((m, ne * ff), BF16),
        compiler_params=_cparams(("arbitrary", "arbitrary")),
        name="moe_up",
    )(xb, comb, wg, bg, wu, bu)


def _moe_down_kernel(h_ref, wd_ref, comb_ref, bd_ref, xf_ref, g_ref, b_ref, of_ref, ob_ref, *, alpha, nk):
    kk = pl.program_id(1)
    tm, d = of_ref.shape
    xw = xf_ref.shape[1]

    @pl.when(kk == 0)
    def _():
        of_ref[...] = jnp.zeros_like(of_ref)

    h = h_ref[...]
    for c in range(0, d, COL_SLICE):
        of_ref[:, c:c + COL_SLICE] += _dot(h, wd_ref[:, c:c + COL_SLICE])
    for j in range(nk):
        @pl.when(kk == j)
        def _(j=j):
            of_ref[:, j * xw:(j + 1) * xw] += alpha * xf_ref[...]

    @pl.when(kk == nk - 1)
    def _():
        for r in range(0, tm, ROW_SLICE):
            sl = slice(r, r + ROW_SLICE)
            y = of_ref[sl, :] + _dot(comb_ref[sl, :].astype(BF16), bd_ref[...])
            y = _ln_rows(y, g_ref[...], b_ref[...])
            of_ref[sl, :] = y
            ob_ref[sl, :] = y.astype(BF16)


def _moe_down(h, wd, layer, comb, bd_pad, xf, g, b, alpha, tm, tk):
    m, d = xf.shape
    nk = h.shape[1] // tk
    once = pl.Buffered(1)
    return pl.pallas_call(
        functools.partial(_moe_down_kernel, alpha=alpha, nk=nk),
        grid=(m // tm, nk),
        in_specs=[pl.BlockSpec((tm, tk), lambda i, kk: (i, kk)),
                  pl.BlockSpec((None, tk, d), lambda i, kk: (layer, kk, 0)),
                  pl.BlockSpec((tm, LANES), lambda i, kk: (i, 0), pipeline_mode=once),
                  pl.BlockSpec((LANES, d), lambda i, kk: (0, 0), pipeline_mode=once),
                  pl.BlockSpec((tm, d // nk), lambda i, kk: (i, kk)),
                  pl.BlockSpec((1, d), lambda i, kk: (0, 0)),
                  pl.BlockSpec((1, d), lambda i, kk: (0, 0))],
        out_specs=[pl.BlockSpec((tm, d), lambda i, kk: (i, 0)),
                   pl.BlockSpec((tm, d), lambda i, kk: (i, 0))],
        out_shape=[jax.ShapeDtypeStruct((m, d), F32), jax.ShapeDtypeStruct((m, d), BF16)],
        compiler_params=_cparams(("arbitrary", "arbitrary")),
        name="moe_down",
    )(h, wd, comb, bd_pad, xf, g.reshape(1, d), b.reshape(1, d))


def _pick(n, prefs):
    for p in prefs:
        if n % p == 0:
            return p
    raise ValueError(f"no tile in {prefs} divides {n}")


def kernel(x, mem, w_in, hgrn_lb_raw, hgrn_norm_g, gdn_conv_w, gdn_a_log, gdn_dt_bias, gdn_norm_g, gla_w_up, gla_b_up, gla_norm_g, w_out, ln_mix_g, ln_mix_b, mem_ln_g, mem_ln_b, xattn_wq, xattn_wkv, xattn_wo, ln_xattn_g, ln_xattn_b, w_router, b_router, w_gate, b_gate, w_up, b_up, w_down, b_down, ln_ffn_g, ln_ffn_b):
    bsz, seq, d = x.shape
    depth = w_in.shape[0]
    n_mem = mem.shape[1]
    ntok = bsz * seq
    alpha = float((2 * depth) ** 0.25)
    assert seq % CHUNK == 0 and d % LANES == 0

    tblk = _pick(seq, (1024, 512, 256, 128, 64))
    group = min(8, tblk // CHUNK)
    tblk_gdn = _pick(seq, (512, 256, 128, 64))
    tm_big = _pick(ntok, (512, 256, 128, 64))
    tm_mid = _pick(ntok, (256, 128, 64))
    tm_up = _pick(ntok, (1024, 512, 256, 128, 64))

    a_w, b_w, c_w = A_HEADS * HEAD_DK, B_HEADS * HEAD_DK, C_HEADS * HEAD_DK
    o_bb = 4 * a_w + (2 * b_w + B_HEADS * B_DV) + B_HEADS * B_DV
    o_cq = o_bb + 2 * B_HEADS
    o_clr = o_cq + 2 * c_w + 2 * C_HEADS * C_DV
    n_cols = o_clr + C_RANK
    assert w_in.shape[2] == n_cols
    assert o_bb == MAIN_COLS
    w_main = w_in[:, :, :o_bb].astype(BF16)
    w_c = w_in[:, :, o_cq:o_clr].astype(BF16)
    w_misc = jnp.concatenate([w_in[:, :, o_bb:o_cq], w_in[:, :, o_clr:],
                              jnp.zeros((depth, d, LANES - 2 * B_HEADS - C_RANK), w_in.dtype)],
                             axis=2).astype(BF16)
    w_out_b = w_out.astype(BF16)
    wq_b, wkv_b, wo_b = xattn_wq.astype(BF16), xattn_wkv.astype(BF16), xattn_wo.astype(BF16)
    wr_p = jnp.concatenate([w_router, jnp.zeros((depth, d, LANES - N_EXPERTS), F32)], axis=2).astype(BF16)
    br_p = jnp.concatenate([b_router, jnp.full((depth, LANES - N_EXPERTS), -1.0e30, F32)], axis=1)
    ne, ff = w_gate.shape[1], w_gate.shape[3]
    wg_r = w_gate.reshape(depth * ne, d, ff)
    wu_r = w_up.reshape(depth * ne, d, ff)
    wd_b = w_down.astype(BF16).reshape(depth, ne * ff, d)
    bg_r = b_gate.reshape(depth * ne, 1, ff)
    bu_r = b_up.reshape(depth * ne, 1, ff)
    bd_p = jnp.concatenate([b_down, jnp.zeros((depth, LANES - N_EXPERTS, d), F32)], axis=1).astype(BF16)
    gla_wup_p = jnp.zeros((depth, LANES, c_w), F32).at[:, MISC_CODE:MISC_CODE + C_RANK, :].set(gla_w_up)
    gla_wup_p = gla_wup_p.reshape(depth, LANES, C_HEADS, HEAD_DK).transpose(0, 2, 1, 3).astype(BF16)
    gla_bup_r = gla_b_up.reshape(depth, C_HEADS, 1, HEAD_DK)
    conv_r = gdn_conv_w.reshape(depth, B_CONV, 3, B_HEADS, HEAD_DK).transpose(0, 3, 2, 1, 4)
    a_log_r = jnp.broadcast_to(gdn_a_log[:, :, None, None], (depth, B_HEADS, 1, LANES))
    dt_r = jnp.broadcast_to(gdn_dt_bias[:, :, None, None], (depth, B_HEADS, 1, LANES))
    lb_all = jnp.cumsum(jax.nn.softmax(hgrn_lb_raw.astype(F32), axis=0), axis=0)
    lb_r = (lb_all - lb_all[0:1]).reshape(depth, A_HEADS, 1, HEAD_DK)
    rsel = _range_selectors()
    masks = _level_masks()

    tm_mem = _pick(bsz * n_mem, (256, 128, 64, 8))
    mem_n = _layer_norm(mem.reshape(bsz * n_mem, d), mem_ln_g, mem_ln_b, BF16, tm_mem)

    xf = x.reshape(ntok, d)
    xb = xf.astype(BF16)
    for l in range(depth):
        proj = _matmul(xb, w_main, l, F32, tm_big, 1280)
        proj_c = _matmul(xb, w_c, l, F32, tm_big, 1536)
        misc = _matmul(xb, w_misc, l, F32, tm_big, LANES)
        oa = _hgrn(proj, lb_r[l], hgrn_norm_g[l], rsel, masks, bsz, seq, tblk, group)
        ob = _gdn(proj, misc, conv_r[l], a_log_r[l], dt_r[l], gdn_norm_g[l], bsz, seq, tblk_gdn, group)
        oc = _gla(proj_c, misc, gla_wup_p[l], gla_bup_r[l], gla_norm_g[l], rsel, masks, bsz, seq, tblk, group)
        xf, xb = _proj_ln(oa, ob, oc, w_out_b, l, xf, ln_mix_g[l], ln_mix_b[l], alpha, tm_big, 512)

        kv = _matmul(mem_n, wkv_b, l, BF16, tm_mem, 2 * X_HEADS * X_HEAD_DIM)
        xf, xb = _xattn(xf, wq_b, kv.reshape(bsz, n_mem, -1), wo_b, l, ln_xattn_g[l], ln_xattn_b[l],
                        alpha, bsz, seq, _pick(seq, (512, 256, 128, 64)))

        comb = _router(xb, wr_p[l], br_p[l, None, :], tm_big)
        hid = _moe_up(xb, comb, wg_r, bg_r, wu_r, bu_r, l, ne, tm_up)
        xf, xb = _moe_down(hid, wd_b, l, comb, bd_p[l], xf, ln_ffn_g[l], ln_ffn_b[l], alpha, tm_big, 1024)
    return xf.reshape(bsz, seq, d)
```

```python
import functools

import numpy as np
import jax
import jax.numpy as jnp
from jax import lax
from jax.experimental import pallas as pl
from jax.experimental.pallas import tpu as pltpu

F32 = jnp.float32
BF16 = jnp.bfloat16

LANES = 128
CHUNK = 64
VMEM_LIMIT = 62 * 1024 * 1024
ROW_SLICE = 256
COL_SLICE = 1024
ATTN_ROWS = 128

A_HEADS, B_HEADS, C_HEADS = 8, 12, 6
HEAD_DK = 128
A_DV, B_DV, C_DV = 128, 128, 256
B_CONV = 4
GDN_HEADS_PER_STEP = 4
DIAG_HEADS_PER_STEP = 2
C_RANK = 16
C_TAU = 16.0
X_HEADS, X_HEAD_DIM = 4, 128
N_EXPERTS, TOP_K, EXPERT_FF = 32, 4, 256
SWIGLU_LIMIT, SWIGLU_ALPHA = 7.0, 1.702
LN_EPS, RMS_EPS = 1e-5, 1e-6

COL_AQ, COL_AF, COL_AI, COL_AG = 0, 8, 16, 24
COL_BQ, COL_BK, COL_BV, COL_BZ = 32, 44, 56, 68
MAIN_COLS = 80 * LANES
COL_CQ, COL_CK, COL_CV, COL_CR = 0, 6, 12, 24
MISC_BETA, MISC_DECAY, MISC_CODE = 0, 12, 24


def _cparams(sem):
    return pltpu.CompilerParams(dimension_semantics=sem, vmem_limit_bytes=VMEM_LIMIT)


def _dot(a, b):
    return jnp.dot(a, b, preferred_element_type=F32)


def _dot_nt(a, b):
    return lax.dot_general(a, b, (((1,), (1,)), ((), ())), preferred_element_type=F32)


def _dot_tn(a, b):
    return lax.dot_general(a, b, (((0,), (0,)), ((), ())), preferred_element_type=F32)


def _split(x):
    hi = x.astype(BF16)
    lo = (x - hi.astype(F32)).astype(BF16)
    return hi, lo


def _dot_exact_lhs(sel2, x):
    hi, lo = _split(x)
    return _dot(sel2, jnp.concatenate([hi, lo], axis=0))


def _sigmoid(x):
    return 1.0 / (1.0 + jnp.exp(-x))


def _silu(x):
    return x * _sigmoid(x)


def _iota(shape, dim):
    return lax.broadcasted_iota(jnp.int32, shape, dim)


def _full_spec(shape):
    return pl.BlockSpec(shape, lambda *_: (0,) * len(shape))


def _mm_kernel(a_ref, w_ref, o_ref):
    o_ref[...] = _dot(a_ref[...], w_ref[...]).astype(o_ref.dtype)


def _matmul(a, w, layer, out_dtype, tm, tn):
    m, k = a.shape
    n = w.shape[2]
    return pl.pallas_call(
        _mm_kernel,
        grid=(n // tn, m // tm),
        in_specs=[pl.BlockSpec((tm, k), lambda j, i: (i, 0)),
                  pl.BlockSpec((None, k, tn), lambda j, i: (layer, 0, j))],
        out_specs=pl.BlockSpec((tm, tn), lambda j, i: (i, j)),
        out_shape=jax.ShapeDtypeStruct((m, n), out_dtype),
        compiler_params=_cparams(("arbitrary", "arbitrary")),
        name="matmul",
    )(a, w)


def _ln_rows(y, g, b):
    mu = jnp.mean(y, axis=-1, keepdims=True)
    d = y - mu
    var = jnp.mean(d * d, axis=-1, keepdims=True)
    return d * lax.rsqrt(var + LN_EPS) * g + b


def _ln_kernel(x_ref, g_ref, b_ref, o_ref):
    o_ref[...] = _ln_rows(x_ref[...], g_ref[...], b_ref[...]).astype(o_ref.dtype)


def _layer_norm(x, g, b, out_dtype, tm):
    m, d = x.shape
    return pl.pallas_call(
        _ln_kernel,
        grid=(m // tm,),
        in_specs=[pl.BlockSpec((tm, d), lambda i: (i, 0)),
                  pl.BlockSpec((1, d), lambda i: (0, 0)),
                  pl.BlockSpec((1, d), lambda i: (0, 0))],
        out_specs=pl.BlockSpec((tm, d), lambda i: (i, 0)),
        out_shape=jax.ShapeDtypeStruct((m, d), out_dtype),
        compiler_params=_cparams(("arbitrary",)),
        name="layer_norm",
    )(x, g.reshape(1, d), b.reshape(1, d))


def _proj_ln_kernel(a0_ref, a1_ref, a2_ref, w_ref, x_ref, g_ref, b_ref, of_ref, *, alpha, bounds):
    kk = pl.program_id(1)
    n0, n1, nk = bounds
    tm, d = of_ref.shape
    rows = [slice(r, r + ROW_SLICE) for r in range(0, tm, ROW_SLICE)]
    cols = [slice(c, c + COL_SLICE) for c in range(0, d, COL_SLICE)]

    @pl.when(kk == 0)
    def _():
        of_ref[...] = jnp.zeros_like(of_ref)

    def accumulate(a_ref):
        a = a_ref[...]
        for sl in cols:
            of_ref[:, sl] += _dot(a, w_ref[:, sl])

    pl.when(kk < n0)(lambda: accumulate(a0_ref))
    pl.when((kk >= n0) & (kk < n1))(lambda: accumulate(a1_ref))
    pl.when(kk >= n1)(lambda: accumulate(a2_ref))

    @pl.when(kk == nk - 1)
    def _():
        for sl in rows:
            of_ref[sl, :] = _ln_rows(alpha * x_ref[sl, :] + of_ref[sl, :], g_ref[...], b_ref[...])


def _proj_ln(a0, a1, a2, w, layer, x, g, b, alpha, tm, tk):
    m = a0.shape[0]
    d = w.shape[2]
    n0 = a0.shape[1] // tk
    n1 = n0 + a1.shape[1] // tk
    nk = n1 + a2.shape[1] // tk
    assert nk * tk == w.shape[1]
    return pl.pallas_call(
        functools.partial(_proj_ln_kernel, alpha=alpha, bounds=(n0, n1, nk)),
        grid=(m // tm, nk),
        in_specs=[pl.BlockSpec((tm, tk), lambda i, kk: (i, jnp.minimum(kk, n0 - 1))),
                  pl.BlockSpec((tm, tk), lambda i, kk: (i, jnp.clip(kk - n0, 0, n1 - n0 - 1))),
                  pl.BlockSpec((tm, tk), lambda i, kk: (i, jnp.clip(kk - n1, 0, nk - n1 - 1))),
                  pl.BlockSpec((None, tk, d), lambda i, kk: (layer, kk, 0)),
                  pl.BlockSpec((tm, d), lambda i, kk: (i, 0)),
                  pl.BlockSpec((1, d), lambda i, kk: (0, 0)),
                  pl.BlockSpec((1, d), lambda i, kk: (0, 0))],
        out_specs=pl.BlockSpec((tm, d), lambda i, kk: (i, 0)),
        out_shape=jax.ShapeDtypeStruct((m, d), F32),
        compiler_params=_cparams(("arbitrary", "arbitrary")),
        name="proj_ln",
    )(a0, a1, a2, w, x, g.reshape(1, d), b.reshape(1, d))


_LEVELS = (32, 16, 8, 4, 2, 1)
_N_SEL = len(_LEVELS) + 2


def _range_selectors():
    i = np.arange(CHUNK)[:, None]
    t = np.arange(CHUNK)[None, :]
    blocks = [t <= i]
    for s in _LEVELS:
        m = (i & ~(2 * s - 1)) | (s - 1)
        later = (i & s) != 0
        blocks.append(np.where(later, (t > m) & (t <= i), (t > i) & (t <= m)))
    blocks.append(t > i)
    sel = np.concatenate(blocks, axis=0).astype(np.float32)
    return jnp.asarray(np.concatenate([sel, sel], axis=1), dtype=BF16)


def _level_masks():
    i = np.arange(CHUNK)[:, None]
    j = np.arange(CHUNK)[None, :]
    masks = [i == j]
    for s in _LEVELS:
        sh = s.bit_length()
        masks.append(((i >> sh) == (j >> sh)) & ((i & s) != 0) & ((j & s) == 0))
    return jnp.asarray(np.stack(masks).astype(np.float32))


def _diag_gated_block(get_q, get_k, get_v, get_la, e_ref, rsel_ref, mask_ref, states_t, nchunks, group, emits):
    nh = len(states_t)
    states_t = list(states_t)
    dk = e_ref.shape[2]
    row = _iota((CHUNK, dk), 0)
    laters = [(row & s) != 0 for s in _LEVELS]
    rsel = rsel_ref[...]
    span = _N_SEL * CHUNK
    for g0 in range(0, nchunks, group):
        items = [(hd, c) for c in range(g0, min(g0 + group, nchunks)) for hd in range(nh)]
        rows = [slice(c * CHUNK, (c + 1) * CHUNK) for _, c in items]
        for (hd, c), sl in zip(items, rows):
            e_ref[hd, c * span:(c + 1) * span, :] = _dot_exact_lhs(rsel, get_la(hd, sl))

        def e_rows(hd, c, first, last):
            return e_ref[hd, c * span + first:c * span + last, :]

        qs = [get_q(hd, sl) for (hd, _), sl in zip(items, rows)]
        ks = [get_k(hd, sl) for (hd, _), sl in zip(items, rows)]
        vbs = [get_v(hd, sl).astype(BF16) for (hd, _), sl in zip(items, rows)]
        scores = [_dot_nt(q.astype(BF16), k.astype(BF16)) * mask_ref[0] for q, k in zip(qs, ks)]
        for n in range(len(_LEVELS)):
            zs = [(jnp.where(laters[n], q, k)
                   * jnp.exp(e_rows(hd, c, (n + 1) * CHUNK, (n + 2) * CHUNK))).astype(BF16)
                  for (hd, c), q, k in zip(items, qs, ks)]
            scores = [sc + _dot_nt(z, z) * mask_ref[n + 1] for sc, z in zip(scores, zs)]
        o_intra = [_dot(sc.astype(BF16), vb) for sc, vb in zip(scores, vbs)]
        q_dec = [(q * jnp.exp(e_rows(hd, c, 0, CHUNK))).astype(BF16) for (hd, c), q in zip(items, qs)]
        k_dec = [(k * jnp.exp(e_rows(hd, c, 7 * CHUNK, span))).astype(BF16) for (hd, c), k in zip(items, ks)]
        ds_t = [_dot_tn(vb, kd) for vb, kd in zip(vbs, k_dec)]
        for n, (hd, c) in enumerate(items):
            emits[hd](c, o_intra[n] + _dot_nt(q_dec[n], states_t[hd].astype(BF16)))
            decay = jnp.exp(e_rows(hd, c, CHUNK - 1, CHUNK))
            states_t[hd] = states_t[hd] * decay + ds_t[n]
    return states_t


def _rms_gate(o, g, gate):
    y = o * lax.rsqrt(jnp.mean(o * o, axis=-1, keepdims=True) + RMS_EPS)
    return y * g * _silu(gate)


def _hgrn_kernel(q_ref, f_ref, i_ref, g_ref, lb_ref, ng_ref, rsel_ref, mask_ref, o_ref,
                 st_ref, qs_ref, ks_ref, la_ref, e_ref, *, nchunks, group):
    @pl.when(pl.program_id(2) == 0)
    def _():
        st_ref[...] = jnp.zeros_like(st_ref)

    ng = ng_ref[...]
    lanes = [slice(hd * LANES, (hd + 1) * LANES) for hd in range(DIAG_HEADS_PER_STEP)]
    emits = []
    for hd, ln in enumerate(lanes):
        lb = lb_ref[hd]
        f = lb + (1.0 - lb) * _sigmoid(f_ref[:, ln])
        qs_ref[hd] = _silu(q_ref[:, ln])
        ks_ref[hd] = 1.0 - f
        la_ref[hd] = jnp.log(f)

        def emit(c, o, ln=ln):
            sl = slice(c * CHUNK, (c + 1) * CHUNK)
            o_ref[sl, ln] = _rms_gate(o, ng, g_ref[sl, ln]).astype(o_ref.dtype)

        emits.append(emit)

    states = _diag_gated_block(lambda hd, sl: qs_ref[hd, sl, :], lambda hd, sl: ks_ref[hd, sl, :],
                               lambda hd, sl: i_ref[sl, lanes[hd]], lambda hd, sl: la_ref[hd, sl, :],
                               e_ref, rsel_ref, mask_ref, [st_ref[hd] for hd in range(len(lanes))],
                               nchunks, group, emits)
    for hd, st in enumerate(states):
        st_ref[hd] = st


def _hgrn(proj, lb, norm_g, rsel, masks, bsz, seq, tblk, group):
    nt = seq // tblk
    hps = DIAG_HEADS_PER_STEP
    width = hps * LANES

    def col(base):
        return pl.BlockSpec((tblk, width), lambda b, p, t: (b * nt + t, base // hps + p))

    return pl.pallas_call(
        functools.partial(_hgrn_kernel, nchunks=tblk // CHUNK, group=group),
        grid=(bsz, A_HEADS // hps, nt),
        in_specs=[col(COL_AQ), col(COL_AF), col(COL_AI), col(COL_AG),
                  pl.BlockSpec((hps, 1, LANES), lambda b, p, t: (p, 0, 0)),
                  _full_spec((1, A_DV)), _full_spec(rsel.shape), _full_spec(masks.shape)],
        out_specs=pl.BlockSpec((tblk, width), lambda b, p, t: (b * nt + t, p)),
        out_shape=jax.ShapeDtypeStruct((bsz * seq, A_HEADS * A_DV), BF16),
        scratch_shapes=[pltpu.VMEM((hps, A_DV, HEAD_DK), F32), pltpu.VMEM((hps, tblk, HEAD_DK), F32),
                        pltpu.VMEM((hps, tblk, HEAD_DK), F32), pltpu.VMEM((hps, tblk, HEAD_DK), F32),
                        pltpu.VMEM((hps, tblk * _N_SEL, HEAD_DK), F32)],
        compiler_params=_cparams(("arbitrary", "arbitrary", "arbitrary")),
        name="hgrn2",
    )(proj, proj, proj, proj, lb, norm_g.reshape(1, A_DV), rsel, masks)


def _gla_kernel(q_ref, k_ref, v_ref, r_ref, misc_ref, wup_ref, bup_ref, ng_ref, rsel_ref, mask_ref, o_ref,
                st_ref, qs_ref, la_ref, e_ref, *, nchunks, group):
    @pl.when(pl.program_id(2) == 0)
    def _():
        st_ref[...] = jnp.zeros_like(st_ref)

    ng = ng_ref[...]
    code = misc_ref[...].astype(BF16)
    nh = DIAG_HEADS_PER_STEP
    lanes = [slice(hd * LANES, (hd + 1) * LANES) for hd in range(nh)]
    vlanes = [slice(hd * C_DV, (hd + 1) * C_DV) for hd in range(nh)]
    emits = []
    for hd in range(nh):
        z = _dot(code, wup_ref[hd]) + bup_ref[hd]
        la_ref[hd] = (jnp.minimum(z, 0.0) - jnp.log(1.0 + jnp.exp(-jnp.abs(z)))) * (1.0 / C_TAU)
        qs_ref[hd] = q_ref[:, lanes[hd]] * (HEAD_DK ** -0.5)

        def emit(c, o, vl=vlanes[hd]):
            sl = slice(c * CHUNK, (c + 1) * CHUNK)
            o_ref[sl, vl] = _rms_gate(o, ng, r_ref[sl, vl]).astype(o_ref.dtype)

        emits.append(emit)

    states = _diag_gated_block(lambda hd, sl: qs_ref[hd, sl, :], lambda hd, sl: k_ref[sl, lanes[hd]],
                               lambda hd, sl: v_ref[sl, vlanes[hd]], lambda hd, sl: la_ref[hd, sl, :],
                               e_ref, rsel_ref, mask_ref, [st_ref[hd] for hd in range(nh)],
                               nchunks, group, emits)
    for hd, st in enumerate(states):
        st_ref[hd] = st


def _gla(proj, misc, w_up_pad, b_up, norm_g, rsel, masks, bsz, seq, tblk, group):
    nt = seq // tblk
    hps = DIAG_HEADS_PER_STEP

    def col(base, per_head):
        width = hps * per_head
        return pl.BlockSpec((tblk, width), lambda b, p, t: (b * nt + t, (base * LANES) // width + p))

    return pl.pallas_call(
        functools.partial(_gla_kernel, nchunks=tblk // CHUNK, group=group),
        grid=(bsz, C_HEADS // hps, nt),
        in_specs=[col(COL_CQ, LANES), col(COL_CK, LANES), col(COL_CV, C_DV), col(COL_CR, C_DV),
                  pl.BlockSpec((tblk, LANES), lambda b, p, t: (b * nt + t, 0)),
                  pl.BlockSpec((hps, LANES, LANES), lambda b, p, t: (p, 0, 0)),
                  pl.BlockSpec((hps, 1, LANES), lambda b, p, t: (p, 0, 0)),
                  _full_spec((1, C_DV)), _full_spec(rsel.shape), _full_spec(masks.shape)],
        out_specs=pl.BlockSpec((tblk, hps * C_DV), lambda b, p, t: (b * nt + t, p)),
        out_shape=jax.ShapeDtypeStruct((bsz * seq, C_HEADS * C_DV), BF16),
        scratch_shapes=[pltpu.VMEM((hps, C_DV, HEAD_DK), F32), pltpu.VMEM((hps, tblk, HEAD_DK), F32),
                        pltpu.VMEM((hps, tblk, HEAD_DK), F32), pltpu.VMEM((hps, tblk * _N_SEL, HEAD_DK), F32)],
        compiler_params=_cparams(("arbitrary", "arbitrary", "arbitrary")),
        name="gla",
    )(proj, proj, proj, proj, misc, w_up_pad, b_up, norm_g.reshape(1, C_DV), rsel, masks)


def _times_p(lefts, p):
    n = p.shape[0]
    ph, pl_ = _split(p)
    halves = [_split(x) for x in lefts]
    by_hi = _dot(jnp.concatenate([h for pair in halves for h in pair], axis=0), ph)
    by_lo = _dot(jnp.concatenate([hi for hi, _ in halves], axis=0), pl_)
    return [by_hi[2 * i * n:(2 * i + 1) * n] + by_hi[(2 * i + 1) * n:(2 * i + 2) * n] + by_lo[i * n:(i + 1) * n]
            for i in range(len(lefts))]


def _unit_lower_inverses(neg_ms):
    shape = neg_ms[0].shape
    eye = jnp.where(_iota(shape, 0) == _iota(shape, 1), 1.0, 0.0)
    ts = [eye + m for m in neg_ms]
    ps = [_times_p([m], m)[0] for m in neg_ms]
    for _ in range(4):
        prods = [_times_p([t, p], p) for t, p in zip(ts, ps)]
        ts = [t + tp for t, (tp, _) in zip(ts, prods)]
        ps = [pp for _, pp in prods]
    return [t + _times_p([t], p)[0] for t, p in zip(ts, ps)]


def _gdn_block(buf_ref, states, nchunks, group, emits):
    nh = len(states)
    states = list(states)
    ri = _iota((CHUNK, CHUNK), 0)
    ci = _iota((CHUNK, CHUNK), 1)
    incl = ri >= ci
    strict = ri > ci
    tri = jnp.where(_iota((CHUNK, 2 * CHUNK), 0) >= (_iota((CHUNK, 2 * CHUNK), 1) & (CHUNK - 1)),
                    1.0, 0.0).astype(BF16)
    upper = jnp.where(strict, 1.0, 0.0)
    for g0 in range(0, nchunks, group):
        items = [(hd, c) for c in range(g0, min(g0 + group, nchunks)) for hd in range(nh)]
        sls = [(hd, slice(c * CHUNK, (c + 1) * CHUNK)) for hd, c in items]
        qs = [buf_ref[hd, 3, sl, :] for hd, sl in sls]
        ks = [buf_ref[hd, 4, sl, :] for hd, sl in sls]
        vs = [buf_ref[hd, 5, sl, :] for hd, sl in sls]
        betas = [buf_ref[hd, 6, sl, :] for hd, sl in sls]
        gs = [buf_ref[hd, 7, sl, :] for hd, sl in sls]
        gcs = [_dot_exact_lhs(tri, g) for g in gs]
        dmats = [_dot_exact_lhs(tri, g[:, :CHUNK] * upper) for g in gs]
        gammas = [jnp.where(incl, jnp.exp(jnp.where(incl, dm, 0.0)), 0.0) for dm in dmats]
        qbs = [q.astype(BF16) for q in qs]
        kbs = [k.astype(BF16) for k in ks]
        kks = [_dot_nt(kb, kb) for kb in kbs]
        ts = _unit_lower_inverses([-jnp.where(strict, beta[:, :CHUNK] * kk * gamma, 0.0)
                                   for beta, kk, gamma in zip(betas, kks, gammas)])
        egcs = [jnp.exp(gc) for gc in gcs]
        rhss = [jnp.concatenate([k * (beta * egc), v * beta], axis=1).astype(BF16)
                for k, v, beta, egc in zip(ks, vs, betas, egcs)]
        wus = []
        for t, rhs in zip(ts, rhss):
            both = _dot(jnp.concatenate(_split(t), axis=0), rhs)
            wus.append((both[:CHUNK] + both[CHUNK:]).astype(BF16))
        attns = [jnp.where(incl, _dot_nt(qb, kb) * gamma, 0.0).astype(BF16)
                 for qb, kb, gamma in zip(qbs, kbs, gammas)]
        gls = [gc[CHUNK - 1:CHUNK, :] for gc in gcs]
        k_decs = [(k * jnp.exp(gl - gc)).astype(BF16) for k, gl, gc in zip(ks, gls, gcs)]
        kwus = [_dot_tn(kd, wu) for kd, wu in zip(k_decs, wus)]
        awus = [_dot(a, wu) for a, wu in zip(attns, wus)]
        q_effs = [(q * egc - awu[:, :HEAD_DK]).astype(BF16) for q, egc, awu in zip(qs, egcs, awus)]
        for n, (hd, c) in enumerate(items):
            st_b = states[hd].astype(BF16)
            emits[hd](c, _dot(q_effs[n], st_b) + awus[n][:, HEAD_DK:])
            states[hd] = (states[hd] * jnp.exp(gls[n]) - _dot(kwus[n][:, :HEAD_DK].astype(BF16), st_b)
                          + kwus[n][:, HEAD_DK:])
    return states


def _l2n(x):
    return x * lax.rsqrt(jnp.sum(x * x, axis=-1, keepdims=True) + RMS_EPS)


def _gdn_kernel(q_ref, k_ref, v_ref, z_ref, misc_ref, cw_ref, al_ref, dt_ref, ng_ref, o_ref,
                st_ref, buf_ref, *, nchunks, tblk, group):
    first_head = pl.program_id(1) * GDN_HEADS_PER_STEP

    @pl.when(pl.program_id(2) == 0)
    def _():
        st_ref[...] = jnp.zeros_like(st_ref)
        buf_ref[...] = jnp.zeros_like(buf_ref)

    mh, ml = _split(misc_ref[...])
    lane_r = _iota((LANES, LANES), 0)
    ng = ng_ref[...]
    emits = []
    for hd in range(GDN_HEADS_PER_STEP):
        lanes = slice(hd * LANES, (hd + 1) * LANES)
        conv = []
        for n, ref in enumerate((q_ref, k_ref, v_ref)):
            buf_ref[hd, n, 8:, :] = ref[:, lanes]
            acc = jnp.zeros((tblk, LANES), F32)
            for j in range(B_CONV):
                lo = 8 - (B_CONV - 1) + j
                acc = acc + cw_ref[hd, n, j:j + 1, :] * buf_ref[hd, n, lo:lo + tblk, :]
            buf_ref[hd, n, 0:8, :] = buf_ref[hd, n, tblk:tblk + 8, :]
            conv.append(_silu(acc))
        buf_ref[hd, 3, 0:tblk, :] = _l2n(conv[0]) * (HEAD_DK ** -0.5)
        buf_ref[hd, 4, 0:tblk, :] = _l2n(conv[1])
        buf_ref[hd, 5, 0:tblk, :] = conv[2]

        sel_beta = jnp.where(lane_r == MISC_BETA + first_head + hd, 1.0, 0.0).astype(BF16)
        sel_dec = jnp.where(lane_r == MISC_DECAY + first_head + hd, 1.0, 0.0).astype(BF16)
        beta_logit = _dot(mh, sel_beta) + _dot(ml, sel_beta)
        dec_in = _dot(mh, sel_dec) + _dot(ml, sel_dec) + dt_ref[hd]
        softplus = jnp.maximum(dec_in, 0.0) + jnp.log(1.0 + jnp.exp(-jnp.abs(dec_in)))
        buf_ref[hd, 6, 0:tblk, :] = _sigmoid(beta_logit)
        buf_ref[hd, 7, 0:tblk, :] = -jnp.exp(al_ref[hd]) * softplus

        def emit(c, o, lanes=lanes):
            sl = slice(c * CHUNK, (c + 1) * CHUNK)
            o_ref[sl, lanes] = _rms_gate(o, ng, z_ref[sl, lanes]).astype(o_ref.dtype)

        emits.append(emit)

    states = _gdn_block(buf_ref, [st_ref[hd] for hd in range(GDN_HEADS_PER_STEP)], nchunks, group, emits)
    for hd in range(GDN_HEADS_PER_STEP):
        st_ref[hd] = states[hd]


def _gdn(proj, misc, conv_w, a_log, dt_bias, norm_g, bsz, seq, tblk, group):
    nt = seq // tblk
    hps = GDN_HEADS_PER_STEP
    width = hps * LANES

    def col(base):
        return pl.BlockSpec((tblk, width), lambda b, p, t: (b * nt + t, base // hps + p))

    def per_group(shape):
        return pl.BlockSpec((None, hps) + shape, lambda b, p, t: (p, 0) + (0,) * len(shape))

    return pl.pallas_call(
        functools.partial(_gdn_kernel, nchunks=tblk // CHUNK, tblk=tblk, group=group),
        grid=(bsz, B_HEADS // hps, nt),
        in_specs=[col(COL_BQ), col(COL_BK), col(COL_BV), col(COL_BZ),
                  pl.BlockSpec((tblk, LANES), lambda b, p, t: (b * nt + t, 0)),
                  per_group((3, B_CONV, LANES)), per_group((1, LANES)), per_group((1, LANES)),
                  _full_spec((1, B_DV))],
        out_specs=pl.BlockSpec((tblk, width), lambda b, p, t: (b * nt + t, p)),
        out_shape=jax.ShapeDtypeStruct((bsz * seq, B_HEADS * B_DV), BF16),
        scratch_shapes=[pltpu.VMEM((hps, HEAD_DK, B_DV), F32), pltpu.VMEM((hps, 8, tblk + 8, LANES), F32)],
        compiler_params=_cparams(("arbitrary", "arbitrary", "arbitrary")),
        name="gdn",
    )(proj, proj, proj, proj, misc,
      conv_w.reshape((B_HEADS // hps, hps) + conv_w.shape[1:]),
      a_log.reshape(B_HEADS // hps, hps, 1, LANES), dt_bias.reshape(B_HEADS // hps, hps, 1, LANES),
      norm_g.reshape(1, B_DV))


def _xattn_kernel(xf_ref, wq_ref, kv_ref, wo_ref, g_ref, b_ref, of_ref, ob_ref, q_ref, o_ref, *, alpha):
    tq, d = of_ref.shape
    width = X_HEADS * X_HEAD_DIM
    q_ref[...] = _dot(xf_ref[...].astype(BF16), wq_ref[...]).astype(BF16)
    for hd in range(X_HEADS):
        lo = hd * X_HEAD_DIM
        kh = kv_ref[:, lo:lo + X_HEAD_DIM]
        vh = kv_ref[:, width + lo:width + lo + X_HEAD_DIM]
        for r in range(0, tq, ATTN_ROWS):
            s = _dot_nt(q_ref[r:r + ATTN_ROWS, lo:lo + X_HEAD_DIM], kh) * (X_HEAD_DIM ** -0.5)
            s = s - jnp.max(s, axis=-1, keepdims=True)
            p = jnp.exp(s)
            p = p / jnp.sum(p, axis=-1, keepdims=True)
            o_ref[r:r + ATTN_ROWS, lo:lo + X_HEAD_DIM] = _dot(p.astype(BF16), vh).astype(BF16)
    o = o_ref[...]
    for c in range(0, d, COL_SLICE):
        of_ref[:, c:c + COL_SLICE] = _dot(o, wo_ref[:, c:c + COL_SLICE])
    for r in range(0, tq, ROW_SLICE):
        sl = slice(r, r + ROW_SLICE)
        y = _ln_rows(alpha * xf_ref[sl, :] + of_ref[sl, :], g_ref[...], b_ref[...])
        of_ref[sl, :] = y
        ob_ref[sl, :] = y.astype(BF16)


def _xattn(xf, wq, kv, wo, layer, g, b, alpha, bsz, seq, tq):
    d = xf.shape[1]
    nt = seq // tq
    width = X_HEADS * X_HEAD_DIM
    n_mem = kv.shape[1]
    return pl.pallas_call(
        functools.partial(_xattn_kernel, alpha=alpha),
        grid=(bsz, nt),
        in_specs=[pl.BlockSpec((tq, d), lambda bb, t: (bb * nt + t, 0)),
                  pl.BlockSpec((None, d, width), lambda bb, t: (layer, 0, 0), pipeline_mode=pl.Buffered(1)),
                  pl.BlockSpec((None, n_mem, 2 * width), lambda bb, t: (bb, 0, 0)),
                  pl.BlockSpec((None, width, d), lambda bb, t: (layer, 0, 0), pipeline_mode=pl.Buffered(1)),
                  pl.BlockSpec((1, d), lambda bb, t: (0, 0)),
                  pl.BlockSpec((1, d), lambda bb, t: (0, 0))],
        out_specs=[pl.BlockSpec((tq, d), lambda bb, t: (bb * nt + t, 0)),
                   pl.BlockSpec((tq, d), lambda bb, t: (bb * nt + t, 0))],
        out_shape=[jax.ShapeDtypeStruct((bsz * seq, d), F32), jax.ShapeDtypeStruct((bsz * seq, d), BF16)],
        scratch_shapes=[pltpu.VMEM((tq, width), BF16), pltpu.VMEM((tq, width), BF16)],
        compiler_params=_cparams(("arbitrary", "arbitrary")),
        name="xattn",
    )(xf, wq, kv, wo, g.reshape(1, d), b.reshape(1, d))


def _router_kernel(xb_ref, w_ref, b_ref, o_ref):
    logits = _dot(xb_ref[...], w_ref[...]) + b_ref[...]
    lane = _iota(logits.shape, 1).astype(F32)
    neg = jnp.float32(-3.0e38)
    picked = []
    vals = []
    work = logits
    for _ in range(TOP_K):
        mx = jnp.max(work, axis=-1, keepdims=True)
        idx = jnp.min(jnp.where(work == mx, lane, float(LANES)), axis=-1, keepdims=True)
        sel = lane == idx
        picked.append(sel)
        vals.append(mx)
        work = jnp.where(sel, neg, work)
    es = [jnp.exp(vv - vals[0]) for vv in vals]
    inv = 1.0 / (es[0] + es[1] + es[2] + es[3])
    comb = jnp.zeros_like(logits)
    for sel, e in zip(picked, es):
        comb = comb + jnp.where(sel, e * inv, 0.0)
    o_ref[...] = comb


def _router(xb, w_pad, b_pad, tm):
    m, d = xb.shape
    return pl.pallas_call(
        _router_kernel,
        grid=(m // tm,),
        in_specs=[pl.BlockSpec((tm, d), lambda i: (i, 0)),
                  pl.BlockSpec((d, LANES), lambda i: (0, 0)),
                  pl.BlockSpec((1, LANES), lambda i: (0, 0))],
        out_specs=pl.BlockSpec((tm, LANES), lambda i: (i, 0)),
        out_shape=jax.ShapeDtypeStruct((m, LANES), F32),
        compiler_params=_cparams(("arbitrary",)),
        name="router",
    )(xb, w_pad, b_pad)


def _moe_up_kernel(xb_ref, comb_ref, wg_ref, bg_ref, wu_ref, bu_ref, h_ref):
    e = pl.program_id(1)
    sel = jnp.where(_iota((LANES, EXPERT_FF), 0) == e, 1.0, 0.0).astype(BF16)
    xb = xb_ref[...]
    gate = jnp.minimum(_dot(xb, wg_ref[...].astype(BF16)) + bg_ref[...], SWIGLU_LIMIT)
    lin = jnp.clip(_dot(xb, wu_ref[...].astype(BF16)) + bu_ref[...], -SWIGLU_LIMIT, SWIGLU_LIMIT)
    hid = (lin + 1.0) * gate * _sigmoid(SWIGLU_ALPHA * gate)
    ch, cl = _split(comb_ref[...])
    weight = _dot(ch, sel) + _dot(cl, sel)
    h_ref[...] = (hid * weight).astype(BF16)


def _moe_up(xb, comb, wg, bg, wu, bu, layer, ne, tm):
    m, d = xb.shape
    ff = wg.shape[2]
    base = layer * ne
    return pl.pallas_call(
        _moe_up_kernel,
        grid=(m // tm, ne),
        in_specs=[pl.BlockSpec((tm, d), lambda i, e: (i, 0)),
                  pl.BlockSpec((tm, LANES), lambda i, e: (i, 0)),
                  pl.BlockSpec((None, d, ff), lambda i, e: (base + e, 0, 0)),
                  pl.BlockSpec((None, 1, ff), lambda i, e: (base + e, 0, 0)),
                  pl.BlockSpec((None, d, ff), lambda i, e: (base + e, 0, 0)),
                  pl.BlockSpec((None, 1, ff), lambda i, e: (base + e, 0, 0))],
        out_specs=pl.BlockSpec((tm, ff), lambda i, e: (i, e)),
        out_shape=jax.ShapeDtypeStruct((m, ne * ff), BF16),
        compiler_params=_cparams(("arbitrary", "arbitrary")),
        name="moe_up",
    )(xb, comb, wg, bg, wu, bu)


def _moe_down_kernel(h_ref, wd_ref, comb_ref, bd_ref, xf_ref, g_ref, b_ref, of_ref, ob_ref, *, alpha, nk):
    kk = pl.program_id(1)
    tm, d = of_ref.shape
    xw = xf_ref.shape[1]

    @pl.when(kk == 0)
    def _():
        of_ref[...] = jnp.zeros_like(of_ref)

    h = h_ref[...]
    for c in range(0, d, COL_SLICE):
        of_ref[:, c:c + COL_SLICE] += _dot(h, wd_ref[:, c:c + COL_SLICE])
    for j in range(nk):
        @pl.when(kk == j)
        def _(j=j):
            of_ref[:, j * xw:(j + 1) * xw] += alpha * xf_ref[...]

    @pl.when(kk == nk - 1)
    def _():
        for r in range(0, tm, ROW_SLICE):
            sl = slice(r, r + ROW_SLICE)
            y = of_ref[sl, :] + _dot(comb_ref[sl, :].astype(BF16), bd_ref[...])
            y = _ln_rows(y, g_ref[...], b_ref[...])
            of_ref[sl, :] = y
            ob_ref[sl, :] = y.astype(BF16)


def _moe_down(h, wd, layer, comb, bd_pad, xf, g, b, alpha, tm, tk):
    m, d = xf.shape
    nk = h.shape[1] // tk
    once = pl.Buffered(1)
    return pl.pallas_call(
        functools.partial(_moe_down_kernel, alpha=alpha, nk=nk),
        grid=(m // tm, nk),
        in_specs=[pl.BlockSpec((tm, tk), lambda i, kk: (i, kk)),
                  pl.BlockSpec((None, tk, d), lambda i, kk: (layer, kk, 0)),
                  pl.BlockSpec((tm, LANES), lambda i, kk: (i, 0), pipeline_mode=once),
                  pl.BlockSpec((LANES, d), lambda i, kk: (0, 0), pipeline_mode=once),
                  pl.BlockSpec((tm, d // nk), lambda i, kk: (i, kk)),
                  pl.BlockSpec((1, d), lambda i, kk: (0, 0)),
                  pl.BlockSpec((1, d), lambda i, kk: (0, 0))],
        out_specs=[pl.BlockSpec((tm, d), lambda i, kk: (i, 0)),
                   pl.BlockSpec((tm, d), lambda i, kk: (i, 0))],
        out_shape=[jax.ShapeDtypeStruct((m, d), F32), jax.ShapeDtypeStruct((m, d), BF16)],
        compiler_params=_cparams(("arbitrary", "arbitrary")),
        name="moe_down",
    )(h, wd, comb, bd_pad, xf, g.reshape(1, d), b.reshape(1, d))


def _pick(n, prefs):
    for p in prefs:
        if n % p == 0:
            return p
    raise ValueError(f"no tile in {prefs} divides {n}")


def kernel(x, mem, w_in, hgrn_lb_raw, hgrn_norm_g, gdn_conv_w, gdn_a_log, gdn_dt_bias, gdn_norm_g, gla_w_up, gla_b_up, gla_norm_g, w_out, ln_mix_g, ln_mix_b, mem_ln_g, mem_ln_b, xattn_wq, xattn_wkv, xattn_wo, ln_xattn_g, ln_xattn_b, w_router, b_router, w_gate, b_gate, w_up, b_up, w_down, b_down, ln_ffn_g, ln_ffn_b):
    bsz, seq, d = x.shape
    depth = w_in.shape[0]
    n_mem = mem.shape[1]
    ntok = bsz * seq
    alpha = float((2 * depth) ** 0.25)
    assert seq % CHUNK == 0 and d % LANES == 0

    tblk = _pick(seq, (1024, 512, 256, 128, 64))
    group = min(8, tblk // CHUNK)
    tblk_gdn = _pick(seq, (512, 256, 128, 64))
    tm_big = _pick(ntok, (512, 256, 128, 64))
    tm_mid = _pick(ntok, (256, 128, 64))
    tm_up = _pick(ntok, (1024, 512, 256, 128, 64))

    a_w, b_w, c_w = A_HEADS * HEAD_DK, B_HEADS * HEAD_DK, C_HEADS * HEAD_DK
    o_bb = 4 * a_w + (2 * b_w + B_HEADS * B_DV) + B_HEADS * B_DV
    o_cq = o_bb + 2 * B_HEADS
    o_clr = o_cq + 2 * c_w + 2 * C_HEADS * C_DV
    n_cols = o_clr + C_RANK
    assert w_in.shape[2] == n_cols
    assert o_bb == MAIN_COLS
    w_main = w_in[:, :, :o_bb].astype(BF16)
    w_c = w_in[:, :, o_cq:o_clr].astype(BF16)
    w_misc = jnp.concatenate([w_in[:, :, o_bb:o_cq], w_in[:, :, o_clr:],
                              jnp.zeros((depth, d, LANES - 2 * B_HEADS - C_RANK), w_in.dtype)],
                             axis=2).astype(BF16)
    w_out_b = w_out.astype(BF16)
    wq_b, wkv_b, wo_b = xattn_wq.astype(BF16), xattn_wkv.astype(BF16), xattn_wo.astype(BF16)
    wr_p = jnp.concatenate([w_router, jnp.zeros((depth, d, LANES - N_EXPERTS), F32)], axis=2).astype(BF16)
    br_p = jnp.concatenate([b_router, jnp.full((depth, LANES - N_EXPERTS), -1.0e30, F32)], axis=1)
    ne, ff = w_gate.shape[1], w_gate.shape[3]
    wg_r = w_gate.reshape(depth * ne, d, ff)
    wu_r = w_up.reshape(depth * ne, d, ff)
    wd_b = w_down.astype(BF16).reshape(depth, ne * ff, d)
    bg_r = b_gate.reshape(depth * ne, 1, ff)
    bu_r = b_up.reshape(depth * ne, 1, ff)
    bd_p = jnp.concatenate([b_down, jnp.zeros((depth, LANES - N_EXPERTS, d), F32)], axis=1).astype(BF16)
    gla_wup_p = jnp.zeros((depth, LANES, c_w), F32).at[:, MISC_CODE:MISC_CODE + C_RANK, :].set(gla_w_up)
    gla_wup_p = gla_wup_p.reshape(depth, LANES, C_HEADS, HEAD_DK).transpose(0, 2, 1, 3).astype(BF16)
    gla_bup_r = gla_b_up.reshape(depth, C_HEADS, 1, HEAD_DK)
    conv_r = gdn_conv_w.reshape(depth, B_CONV, 3, B_HEADS, HEAD_DK).transpose(0, 3, 2, 1, 4)
    a_log_r = jnp.broadcast_to(gdn_a_log[:, :, None, None], (depth, B_HEADS, 1, LANES))
    dt_r = jnp.broadcast_to(gdn_dt_bias[:, :, None, None], (depth, B_HEADS, 1, LANES))
    lb_all = jnp.cumsum(jax.nn.softmax(hgrn_lb_raw.astype(F32), axis=0), axis=0)
    lb_r = (lb_all - lb_all[0:1]).reshape(depth, A_HEADS, 1, HEAD_DK)
    rsel = _range_selectors()
    masks = _level_masks()

    tm_mem = _pick(bsz * n_mem, (256, 128, 64, 8))
    mem_n = _layer_norm(mem.reshape(bsz * n_mem, d), mem_ln_g, mem_ln_b, BF16, tm_mem)

    xf = x.reshape(ntok, d)
    xb = xf.astype(BF16)
    for l in range(depth):
        proj = _matmul(xb, w_main, l, F32, tm_big, 1280)
        proj_c = _matmul(xb, w_c, l, F32, tm_big, 1536)
        misc = _matmul(xb, w_misc, l, F32, tm_big, LANES)
        oa = _hgrn(proj, lb_r[l], hgrn_norm_g[l], rsel, masks, bsz, seq, tblk, group)
        ob = _gdn(proj, misc, conv_r[l], a_log_r[l], dt_r[l], gdn_norm_g[l], bsz, seq, tblk_gdn, group)
        oc = _gla(proj_c, misc, gla_wup_p[l], gla_bup_r[l], gla_norm_g[l], rsel, masks, bsz, seq, tblk, group)
        xf = _proj_ln(oa, ob, oc, w_out_b, l, xf, ln_mix_g[l], ln_mix_b[l], alpha, tm_big, 512)

        kv = _matmul(mem_n, wkv_b, l, BF16, tm_mem, 2 * X_HEADS * X_HEAD_DIM)
        xf, xb = _xattn(xf, wq_b, kv.reshape(bsz, n_mem, -1), wo_b, l, ln_xattn_g[l], ln_xattn_b[l],
                        alpha, bsz, seq, _pick(seq, (512, 256, 128, 64)))

        comb = _router(xb, wr_p[l], br_p[l, None, :], tm_big)
        hid = _moe_up(xb, comb, wg_r, bg_r, wu_r, bu_r, l, ne, tm_up)
        xf, xb = _moe_down(hid, wd_b, l, comb, bd_p[l], xf, ln_ffn_g[l], ln_ffn_b[l], alpha, tm_big, 1024)
    return xf.reshape(bsz, seq, d)
```
